```python
import jax, jax.numpy as jnp
from jax import lax
import numpy as np

D_MODEL = 1024
BATCH = 32
SEQ = 2048
DEPTH = 1

NORM_EPS = 1e-6
HG_HEADS = 4
HG_DK = 128
HG_DV = 128
HG_F = HG_HEADS * HG_DK
HG_WIDTH = HG_HEADS * HG_DV
HG_CHUNK = 64
MLA_HEADS = 4
MLA_NOPE = 128
MLA_ROPE = 64
MLA_V = 128
MLA_QK = MLA_NOPE + MLA_ROPE
Q_LORA = 384
KV_LORA = 256
MLA_WIDTH = MLA_HEADS * MLA_V
ROPE_THETA = 10000.0
Q_BLOCK = 128
D_MIX = HG_WIDTH + MLA_WIDTH
D_FF = 4 * D_MODEL
IN_SIZES = (HG_F, HG_F, HG_F, HG_WIDTH, HG_WIDTH, Q_LORA, KV_LORA, MLA_ROPE)
D_IN = HG_F * 3 + HG_WIDTH * 2 + Q_LORA + KV_LORA + MLA_ROPE

kernel_name = 'hymba_hgrn2_mla_sqrelu_encoder'


def rms_norm(t, gain):
    tf = t.astype(jnp.float32)
    y = tf * lax.rsqrt(jnp.mean(tf * tf, axis=-1, keepdims=True) + NORM_EPS)
    return (y * gain.astype(jnp.float32)).astype(t.dtype)


def apply_rope(t, cos, sin):
    half = t.shape[-1] // 2
    t1, t2 = t[..., :half], t[..., half:]
    cos = cos.astype(t.dtype)
    sin = sin.astype(t.dtype)
    return jnp.concatenate([t1 * cos - t2 * sin, t2 * cos + t1 * sin], axis=-1)


def rope_tables(positions):
    inv_freq = ROPE_THETA ** (-jnp.arange(0, MLA_ROPE, 2, dtype=jnp.float32) / MLA_ROPE)
    ang = positions.astype(jnp.float32)[..., None] * inv_freq
    return jnp.cos(ang), jnp.sin(ang)


def hgrn2_direction(q, k, v, log_f):
    b_, s_, h_, dk = q.shape
    dv = v.shape[-1]
    nc = s_ // HG_CHUNK

    def to_chunks(t):
        return t.reshape(b_, nc, HG_CHUNK, h_, t.shape[-1]).transpose(1, 0, 3, 2, 4)

    order_mask = jnp.tril(jnp.ones((HG_CHUNK, HG_CHUNK), dtype=bool))

    def step(state, inp):
        q_c, k_c, v_c, g_c = inp
        b = jnp.cumsum(g_c, axis=2)
        o_inter = jnp.einsum('bhtd,bhde->bhte', q_c * jnp.exp(b), state)
        diff = b[:, :, :, None, :] - b[:, :, None, :, :]
        decay = jnp.exp(jnp.where(order_mask[:, :, None], diff, -jnp.inf))
        scores = jnp.einsum('bhtsd,bhsd->bhts', q_c[:, :, :, None, :] * decay, k_c)
        o_intra = jnp.einsum('bhts,bhse->bhte', scores, v_c)
        b_last = b[:, :, -1:, :]
        new_state = (jnp.exp(b_last[:, :, 0, :, None]) * state
                     + jnp.einsum('bhsd,bhse->bhde', k_c * jnp.exp(b_last - b), v_c))
        return new_state, o_inter + o_intra

    state0 = jnp.zeros((b_, h_, dk, dv), jnp.float32)
    _, o = lax.scan(step, state0, (to_chunks(q), to_chunks(k), to_chunks(v), to_chunks(log_f)))
    return o.transpose(1, 0, 3, 2, 4).reshape(b_, s_, h_, dv)


def hgrn2_group(q_lin, ff_lin, fb_lin, i_lin, g_lin, lb, out_gain):
    b_, s_, _ = q_lin.shape
    dtype = q_lin.dtype
    heads_k = lambda t: t.astype(jnp.float32).reshape(b_, s_, HG_HEADS, HG_DK)
    q = heads_k(q_lin)
    v = i_lin.astype(jnp.float32).reshape(b_, s_, HG_HEADS, HG_DV)

    def gate(logit, lb_dir):
        f = lb_dir + (1.0 - lb_dir) * jax.nn.sigmoid(logit.astype(jnp.float32))
        return heads_k(jnp.log(f)), heads_k(1.0 - f)

    logf_f, k_f = gate(ff_lin, lb[0])
    logf_b, k_b = gate(fb_lin, lb[1])
    flip = lambda t: jnp.flip(t, axis=1)
    o_fwd = hgrn2_direction(q, k_f, v, logf_f)
    o_bwd = flip(hgrn2_direction(flip(q), flip(k_b), flip(v), flip(logf_b)))
    o = rms_norm(o_fwd + o_bwd, out_gain)
    o = o.reshape(b_, s_, HG_WIDTH) * jax.nn.silu(g_lin.astype(jnp.float32))
    return o.astype(dtype)


def blocked_attention(q, k, v):
    b_, s_, h_, d_ = q.shape
    nb = s_ // Q_BLOCK
    qb = (q * (MLA_QK ** -0.5)).reshape(b_, nb, Q_BLOCK, h_, d_).transpose(1, 0, 2, 3, 4)

    def one_block(qi):
        s = jnp.einsum('bqhd,bkhd->bhqk', qi, k).astype(jnp.float32)
        p = jax.nn.softmax(s, axis=-1).astype(v.dtype)
        return jnp.einsum('bhqk,bkhe->bqhe', p, v)

    o = lax.map(one_block, qb)
    return o.transpose(1, 0, 2, 3, 4).reshape(b_, s_, h_, v.shape[-1])


def mla_group(c_q, c_kv, k_rope, cos, sin, g_cq, w_q_up, g_ckv, w_kv_up, g_q_norm, g_k_norm, g_out):
    b_, s_, _ = c_q.shape
    q = (rms_norm(c_q, g_cq) @ w_q_up).reshape(b_, s_, MLA_HEADS, MLA_QK)
    kv = (rms_norm(c_kv, g_ckv) @ w_kv_up).reshape(b_, s_, MLA_HEADS, MLA_NOPE + MLA_V)
    k_nope, v = kv[..., :MLA_NOPE], kv[..., MLA_NOPE:]
    k_pe = jnp.broadcast_to(k_rope[:, :, None, :], (b_, s_, MLA_HEADS, MLA_ROPE))
    k = jnp.concatenate([k_nope, k_pe], axis=-1)
    q = rms_norm(q, g_q_norm)
    k = rms_norm(k, g_k_norm)
    cos_h, sin_h = cos[:, :, None, :], sin[:, :, None, :]
    q = jnp.concatenate([q[..., :MLA_NOPE], apply_rope(q[..., MLA_NOPE:], cos_h, sin_h)], axis=-1)
    k = jnp.concatenate([k[..., :MLA_NOPE], apply_rope(k[..., MLA_NOPE:], cos_h, sin_h)], axis=-1)
    o = blocked_attention(q, k, v).reshape(b_, s_, MLA_WIDTH)
    return rms_norm(o, g_out)


def setup_inputs(seed: int = 0) -> dict:
    key = jax.random.key(seed)
    ks = jax.random.split(key, 20)
    f32 = jnp.float32
    nrm = lambda k, shape, fan_in: jax.random.normal(k, shape, f32) * (fan_in ** -0.5)
    gain = lambda k, shape: 1.0 + 0.1 * jax.random.normal(k, shape, f32)
    x = jax.random.normal(ks[0], (BATCH, SEQ, D_MODEL), f32)
    offsets = jax.random.randint(ks[1], (BATCH, 1), 0, 512, dtype=jnp.int32)
    positions = jnp.arange(SEQ, dtype=jnp.int32)[None, :] + offsets
    return {
        'x': x,
        'positions': positions,
        'g_mix_norm': gain(ks[2], (DEPTH, D_MODEL)),
        'w_in': nrm(ks[3], (DEPTH, D_MODEL, D_IN), D_MODEL),
        'lb_param': jax.random.normal(ks[4], (2, DEPTH + 1, HG_F), f32),
        'g_hgrn_out': gain(ks[5], (DEPTH, HG_HEADS, HG_DV)),
        'g_cq': gain(ks[6], (DEPTH, Q_LORA)),
        'w_q_up': nrm(ks[7], (DEPTH, Q_LORA, MLA_HEADS * MLA_QK), Q_LORA),
        'g_ckv': gain(ks[8], (DEPTH, KV_LORA)),
        'w_kv_up': nrm(ks[9], (DEPTH, KV_LORA, MLA_HEADS * (MLA_NOPE + MLA_V)), KV_LORA),
        'g_q_norm': gain(ks[10], (DEPTH, MLA_QK)),
        'g_k_norm': gain(ks[11], (DEPTH, MLA_QK)),
        'g_mla_out': gain(ks[12], (DEPTH, MLA_WIDTH)),
        'w_out': nrm(ks[13], (DEPTH, D_MIX, D_MODEL), D_MIX),
        'g_ffn_norm': gain(ks[14], (DEPTH, D_MODEL)),
        'w_up': nrm(ks[15], (DEPTH, D_MODEL, D_FF), D_MODEL),
        'w_down': nrm(ks[16], (DEPTH, D_FF, D_MODEL), D_FF),
    }


def reference(x, positions, g_mix_norm, w_in, lb_param, g_hgrn_out, g_cq, w_q_up, g_ckv,
              w_kv_up, g_q_norm, g_k_norm, g_mla_out, w_out, g_ffn_norm, w_up, w_down):
    cos, sin = rope_tables(positions)
    lb_table = jnp.cumsum(jax.nn.softmax(lb_param.astype(jnp.float32), axis=1), axis=1)
    for layer in range(DEPTH):
        h = rms_norm(x, g_mix_norm[layer])
        proj = h @ w_in[layer]
        parts = []
        start = 0
        for size in IN_SIZES:
            parts.append(proj[..., start:start + size])
            start += size
        q_hg, ff_hg, fb_hg, i_hg, g_hg, c_q, c_kv, k_rope = parts
        o_a = hgrn2_group(q_hg, ff_hg, fb_hg, i_hg, g_hg, lb_table[:, layer], g_hgrn_out[layer])
        o_b = mla_group(c_q, c_kv, k_rope, cos, sin, g_cq[layer], w_q_up[layer], g_ckv[layer],
                        w_kv_up[layer], g_q_norm[layer], g_k_norm[layer], g_mla_out[layer])
        x = x + jnp.concatenate([o_a, o_b], axis=-1) @ w_out[layer]
        hf = rms_norm(x, g_ffn_norm[layer]) @ w_up[layer]
        x = x + jnp.square(jax.nn.relu(hf)) @ w_down[layer]
    return x
```

```python
import functools

import jax
import jax.numpy as jnp
from jax import lax
from jax.experimental import pallas as pl
from jax.experimental.pallas import tpu as pltpu

NORM_EPS = 1e-6
HG_HEADS = 4
HG_D = 128
HG_W = HG_HEADS * HG_D
MLA_HEADS = 4
MLA_NOPE = 128
MLA_ROPE = 64
MLA_V = 128
MLA_QK = MLA_NOPE + MLA_ROPE
Q_LORA = 384
KV_LORA = 256
ROPE_THETA = 10000.0
LANES = 128
QK_PAD = 2 * LANES

HG_CHUNK = 128
HG_LEAF = 8
VMEM_LIMIT = 56 * 1024 * 1024

F32 = jnp.float32
BF16 = jnp.bfloat16


def _dot(a, b):
    return jnp.dot(a, b, preferred_element_type=F32)


def _dot_nt(a, b):
    return lax.dot_general(a, b, (((1,), (1,)), ((), ())), preferred_element_type=F32)


def _rms(t, gain):
    return t * lax.rsqrt(jnp.mean(t * t, axis=-1, keepdims=True) + NORM_EPS) * gain


def _sigmoid(t):
    return 1.0 / (1.0 + jnp.exp(-t))


def _swap_rope_halves(t):
    lane = lax.broadcasted_iota(jnp.int32, t.shape, 1)
    return jnp.where(lane < MLA_ROPE // 2, pltpu.roll(t, LANES - MLA_ROPE // 2, 1),
                     pltpu.roll(t, MLA_ROPE // 2, 1))


def _input_kernel(x_ref, pos_ref, invf_ref, g_mix_ref, w_qig_ref, w_f_ref, w_mla_ref, g_cq_ref,
                  w_qup_ref, g_ckv_ref, w_kvup_ref, g_qn_ref, g_qr_ref, g_kn_ref, g_kr_ref,
                  qig_ref, f_ref, qa_ref, ka_ref, v_ref):
    x = x_ref[...]
    h = _rms(x, g_mix_ref[...]).astype(BF16)
    qig_ref[...] = _dot(h, w_qig_ref[...]).astype(BF16)
    f_ref[...] = _dot(h, w_f_ref[...])
    m = _dot(h, w_mla_ref[...])
    c_q = m[:, :Q_LORA]
    c_kv = m[:, Q_LORA:Q_LORA + KV_LORA]
    k_r = m[:, Q_LORA + KV_LORA:]
    q = _dot(_rms(c_q, g_cq_ref[...]).astype(BF16), w_qup_ref[...])
    kv = _dot(_rms(c_kv, g_ckv_ref[...]).astype(BF16), w_kvup_ref[...])
    v_ref[...] = kv[:, MLA_HEADS * MLA_NOPE:].astype(BF16)

    ang = invf_ref[...] * pos_ref[...]
    c, s = jnp.cos(ang), jnp.sin(ang)
    z = jnp.zeros_like(c)
    cos_t = jnp.concatenate([c, c, z, z], axis=0).T
    sin_t = jnp.concatenate([-s, s, z, z], axis=0).T

    def rope(t):
        return t * cos_t + _swap_rope_halves(t) * sin_t

    k_pe = rope(k_r * g_kr_ref[...])
    ssq_kr = jnp.sum(k_r * k_r, axis=-1, keepdims=True)
    scale = MLA_QK ** -0.5
    for hd in range(MLA_HEADS):
        lo, hi = hd * LANES, (hd + 1) * LANES
        q_n = q[:, lo:hi]
        q_r = q[:, MLA_HEADS * MLA_NOPE + lo:MLA_HEADS * MLA_NOPE + hi]
        ssq = jnp.sum(q_n * q_n, axis=-1, keepdims=True) + jnp.sum(q_r * q_r, axis=-1, keepdims=True)
        rstd = lax.rsqrt(ssq * (1.0 / MLA_QK) + NORM_EPS) * scale
        q_pe = rope(q_r * g_qr_ref[...])
        qa_ref[hd] = jnp.concatenate([q_n * g_qn_ref[...] * rstd, q_pe * rstd], axis=1).astype(BF16)
        k_n = kv[:, lo:hi]
        ssq = jnp.sum(k_n * k_n, axis=-1, keepdims=True) + ssq_kr
        rstd = lax.rsqrt(ssq * (1.0 / MLA_QK) + NORM_EPS)
        ka_ref[hd] = jnp.concatenate([k_n * g_kn_ref[...] * rstd, k_pe * rstd], axis=1).astype(BF16)


def _const_spec(shape):
    nd = len(shape)
    return pl.BlockSpec(shape, lambda *_: (0,) * nd, pipeline_mode=pl.Buffered(1))


def _input_stage(x2, pos, invf, g_mix, w_qig, w_f, w_mla, g_cq, w_qup, g_ckv, w_kvup,
                 g_qn, g_qr, g_kn, g_kr, tm):
    t_tok, d_model = x2.shape
    consts = (invf, g_mix, w_qig, w_f, w_mla, g_cq, w_qup, g_ckv, w_kvup, g_qn, g_qr, g_kn, g_kr)
    return pl.pallas_call(
        _input_kernel,
        grid=(t_tok // tm,),
        in_specs=[pl.BlockSpec((tm, d_model), lambda i: (i, 0)),
                  pl.BlockSpec((1, tm), lambda i: (0, i))] + [_const_spec(c.shape) for c in consts],
        out_specs=[pl.BlockSpec((tm, 3 * HG_W), lambda i: (i, 0)),
                   pl.BlockSpec((tm, 2 * HG_W), lambda i: (i, 0)),
                   pl.BlockSpec((MLA_HEADS, tm, QK_PAD), lambda i: (0, i, 0)),
                   pl.BlockSpec((MLA_HEADS, tm, QK_PAD), lambda i: (0, i, 0)),
                   pl.BlockSpec((tm, MLA_HEADS * MLA_V), lambda i: (i, 0))],
        out_shape=[jax.ShapeDtypeStruct((t_tok, 3 * HG_W), BF16),
                   jax.ShapeDtypeStruct((t_tok, 2 * HG_W), F32),
                   jax.ShapeDtypeStruct((MLA_HEADS, t_tok, QK_PAD), BF16),
                   jax.ShapeDtypeStruct((MLA_HEADS, t_tok, QK_PAD), BF16),
                   jax.ShapeDtypeStruct((t_tok, MLA_HEADS * MLA_V), BF16)],
        compiler_params=pltpu.CompilerParams(dimension_semantics=("arbitrary",),
                                             vmem_limit_bytes=VMEM_LIMIT),
        name="input_stage",
    )(x2, pos, *consts)


def _hg_levels():
    sizes = []
    half = HG_CHUNK // 2
    while half >= HG_LEAF:
        sizes.append(half)
        half //= 2
    return tuple(sizes)


def _hg_chunk(q, logit, v, lb, st, tri, masks, rev):
    c = HG_CHUNK
    f = lb + (1.0 - lb) * _sigmoid(logit)
    g = jnp.log(f)
    k = 1.0 - f
    g_hi = g.astype(BF16)
    r1 = g - g_hi.astype(F32)
    g_mid = r1.astype(BF16)
    g_lo = (r1 - g_mid.astype(F32)).astype(BF16)
    b = _dot(tri, g_hi) + _dot(tri, g_mid) + _dot(tri, g_lo)
    b_tot = b[0:1] if rev else b[c - 1:c]

    o = _dot_nt((q * jnp.exp(b)).astype(BF16), st.astype(BF16))
    v_t = v.T
    k_out = (k * jnp.exp(b_tot - b)).astype(BF16)
    st_new = jnp.exp(b_tot) * st + _dot(v_t.astype(BF16), k_out)

    neg = jnp.float32(-1e30)
    p = None
    for lvl, half in enumerate(_hg_levels()):
        grp = 2 * half
        n_grp = c // grp
        b3 = b.reshape(n_grp, grp, HG_D)
        row = lax.broadcasted_iota(jnp.int32, (n_grp, grp, HG_D), 1)
        late = (row < half) if rev else (row >= half)
        mid = b3[:, half:half + 1] if rev else b3[:, half - 1:half]
        e_q = jnp.exp(jnp.where(late, b3 - mid, neg)).reshape(c, HG_D)
        e_k = jnp.exp(jnp.where(late, neg, mid - b3)).reshape(c, HG_D)
        p_l = _dot_nt((q * e_q).astype(BF16), (k * e_k).astype(BF16))
        if n_grp > 1:
            p_l = p_l * masks[lvl]
        p = p_l if p is None else p + p_l
    o = o + _dot(p.astype(BF16), v.astype(BF16))

    q_t, k_t, b_t = q.T, k.T, b.T
    pos_in_leaf = lax.broadcasted_iota(jnp.int32, (1, c), 1) % HG_LEAF
    o_t = jnp.sum(q_t * k_t, axis=0, keepdims=True) * v_t
    for d in range(1, HG_LEAF):
        shift = c - d if rev else d
        k_s = pltpu.roll(k_t, shift, 1)
        b_s = pltpu.roll(b_t, shift, 1)
        v_s = pltpu.roll(v_t, shift, 1)
        w = jnp.sum(q_t * k_s * jnp.exp(jnp.minimum(b_t - b_s, 0.0)), axis=0, keepdims=True)
        ok = (pos_in_leaf < HG_LEAF - d) if rev else (pos_in_leaf >= d)
        o_t = o_t + jnp.where(ok, w, 0.0) * v_s
    return o + o_t.T, st_new


def _hgrn_kernel(q_ref, i_ref, g_ref, ff_ref, fb_ref, lbp_ref, gain_ref, tri_ref, masks_ref,
                 out_ref, st_ref, of_ref):
    s_len = q_ref.shape[0]
    n_chunk = s_len // HG_CHUNK
    def lower_bound(d):
        p = lbp_ref[d]
        e = jnp.exp(p - jnp.max(p, axis=0, keepdims=True))
        return e[0:1] / jnp.sum(e, axis=0, keepdims=True)

    masks = [masks_ref[l] for l in range(masks_ref.shape[0])]

    def run(logit_ref, lb_dir, tri, rev, finish):
        st_ref[...] = jnp.zeros_like(st_ref)

        def body(n, carry):
            ci = (n_chunk - 1 - n) if rev else n
            rows = pl.ds(pl.multiple_of(ci * HG_CHUNK, HG_CHUNK), HG_CHUNK)
            o, st_new = _hg_chunk(q_ref[rows, :].astype(F32), logit_ref[rows, :],
                                  i_ref[rows, :].astype(F32), lb_dir, st_ref[...], tri, masks, rev)
            st_ref[...] = st_new
            finish(rows, o)
            return carry

        lax.fori_loop(0, n_chunk, body, 0)

    def keep_forward(rows, o):
        of_ref[rows, :] = o

    def finish_backward(rows, o):
        y = _rms(o + of_ref[rows, :], gain_ref[...])
        gate = g_ref[rows, :].astype(F32)
        out_ref[rows, :] = (y * gate * _sigmoid(gate)).astype(BF16)

    run(ff_ref, lower_bound(0), tri_ref[0], False, keep_forward)
    run(fb_ref, lower_bound(1), tri_ref[1], True, finish_backward)


def _hgrn_stage(qig, fgate, lb_param, gain, batch, s_len):
    t_tok = qig.shape[0]
    c = HG_CHUNK
    r = jnp.arange(c)
    lower = (r[None, :] <= r[:, None])
    tri = jnp.stack([lower, lower.T]).astype(BF16)
    masks = jnp.stack([(r[:, None] // (2 * h) == r[None, :] // (2 * h)) for h in _hg_levels()]).astype(F32)
    blk = lambda col: pl.BlockSpec((s_len, HG_D), lambda b, h, col=col: (b, col * HG_HEADS + h))
    return pl.pallas_call(
        _hgrn_kernel,
        grid=(batch, HG_HEADS),
        in_specs=[blk(0), blk(1), blk(2), blk(0), blk(1),
                  pl.BlockSpec((2, 2, HG_D), lambda b, h: (0, 0, h)),
                  pl.BlockSpec((1, HG_D), lambda b, h: (0, h)),
                  _const_spec(tri.shape), _const_spec(masks.shape)],
        out_specs=pl.BlockSpec((s_len, HG_D), lambda b, h: (b, h)),
        out_shape=jax.ShapeDtypeStruct((t_tok, HG_W), BF16),
        scratch_shapes=[pltpu.VMEM((HG_D, HG_D), F32), pltpu.VMEM((s_len, HG_D), F32)],
        compiler_params=pltpu.CompilerParams(dimension_semantics=("arbitrary", "arbitrary"),
                                             vmem_limit_bytes=VMEM_LIMIT),
        name="hgrn_stage",
    )(qig, qig, qig, fgate, fgate, lb_param, gain, tri, masks)


def _attn_kernel(q_ref, k_ref, v_ref, o_ref):
    s = _dot_nt(q_ref[0], k_ref[0])
    p = jnp.exp(s - jnp.max(s, axis=-1, keepdims=True))
    o = _dot(p.astype(BF16), v_ref[...])
    o_ref[...] = (o / jnp.sum(p, axis=-1, keepdims=True)).astype(BF16)


def _attn_stage(q_all, k_all, v, batch, s_len, tq):
    t_tok = v.shape[0]
    nq = s_len // tq
    return pl.pallas_call(
        _attn_kernel,
        grid=(batch, MLA_HEADS, nq),
        in_specs=[pl.BlockSpec((1, tq, QK_PAD), lambda b, h, i: (h, b * nq + i, 0)),
                  pl.BlockSpec((1, s_len, QK_PAD), lambda b, h, i: (h, b, 0)),
                  pl.BlockSpec((s_len, MLA_V), lambda b, h, i: (b, h))],
        out_specs=pl.BlockSpec((tq, MLA_V), lambda b, h, i: (b * nq + i, h)),
        out_shape=jax.ShapeDtypeStruct((t_tok, MLA_HEADS * MLA_V), BF16),
        compiler_params=pltpu.CompilerParams(dimension_semantics=("arbitrary",) * 3,
                                             vmem_limit_bytes=VMEM_LIMIT),
        name="attn_stage",
    )(q_all, k_all, v)


def _output_kernel(x_ref, a_ref, ob_ref, g_mla_ref, w_oa_ref, w_ob_ref, g_ffn_ref, w_up_ref,
                   w_dn_ref, out_ref):
    ob = _rms(ob_ref[...].astype(F32), g_mla_ref[...]).astype(BF16)
    x1 = x_ref[...] + _dot(a_ref[...], w_oa_ref[...]) + _dot(ob, w_ob_ref[...])
    h = _rms(x1, g_ffn_ref[...]).astype(BF16)
    up = jnp.maximum(_dot(h, w_up_ref[...]), 0.0)
    out_ref[...] = x1 + _dot((up * up).astype(BF16), w_dn_ref[...])


def _output_stage(x2, a, ob, g_mla, w_oa, w_ob, g_ffn, w_up, w_dn, tm):
    t_tok, d_model = x2.shape
    consts = (g_mla, w_oa, w_ob, g_ffn, w_up, w_dn)
    return pl.pallas_call(
        _output_kernel,
        grid=(t_tok // tm,),
        in_specs=[pl.BlockSpec((tm, d_model), lambda i: (i, 0)),
                  pl.BlockSpec((tm, HG_W), lambda i: (i, 0)),
                  pl.BlockSpec((tm, MLA_HEADS * MLA_V), lambda i: (i, 0))]
                 + [_const_spec(c.shape) for c in consts],
        out_specs=pl.BlockSpec((tm, d_model), lambda i: (i, 0)),
        out_shape=jax.ShapeDtypeStruct((t_tok, d_model), F32),
        compiler_params=pltpu.CompilerParams(dimension_semantics=("arbitrary",),
                                             vmem_limit_bytes=VMEM_LIMIT),
        name="output_stage",
    )(x2, a, ob, *consts)


def kernel(x, positions, g_mix_norm, w_in, lb_param, g_hgrn_out, g_cq, w_q_up, g_ckv, w_kv_up,
           g_q_norm, g_k_norm, g_mla_out, w_out, g_ffn_norm, w_up, w_down):
    batch, s_len, d_model = x.shape
    assert g_mix_norm.shape[0] == 1 and lb_param.shape[1] == 2, "one layer"
    assert s_len % HG_CHUNK == 0
    t_tok = batch * s_len
    row = lambda t: t.reshape(1, -1).astype(F32)

    w = w_in[0]
    o_q, o_ff, o_fb, o_i, o_g, o_cq = (HG_W * n for n in range(6))
    o_ckv = o_cq + Q_LORA
    o_kr = o_ckv + KV_LORA
    w_qig = jnp.concatenate([w[:, o_q:o_ff], w[:, o_i:o_g], w[:, o_g:o_cq]], axis=1).astype(BF16)
    w_f = w[:, o_ff:o_i].astype(BF16)
    w_mla = jnp.pad(w[:, o_cq:], ((0, 0), (0, LANES - MLA_ROPE))).astype(BF16)

    wq = w_q_up[0].reshape(Q_LORA, MLA_HEADS, MLA_QK)
    wq_rope = jnp.pad(wq[:, :, MLA_NOPE:], ((0, 0), (0, 0), (0, LANES - MLA_ROPE)))
    w_qup = jnp.concatenate([wq[:, :, :MLA_NOPE].reshape(Q_LORA, -1),
                             wq_rope.reshape(Q_LORA, -1)], axis=1).astype(BF16)
    wkv = w_kv_up[0].reshape(KV_LORA, MLA_HEADS, MLA_NOPE + MLA_V)
    w_kvup = jnp.concatenate([wkv[:, :, :MLA_NOPE].reshape(KV_LORA, -1),
                              wkv[:, :, MLA_NOPE:].reshape(KV_LORA, -1)], axis=1).astype(BF16)
    pad_rope = lambda t: jnp.pad(t, (0, LANES - MLA_ROPE)).reshape(1, LANES).astype(F32)
    g_qn, g_qr = row(g_q_norm[0, :MLA_NOPE]), pad_rope(g_q_norm[0, MLA_NOPE:])
    g_kn, g_kr = row(g_k_norm[0, :MLA_NOPE]), pad_rope(g_k_norm[0, MLA_NOPE:])
    invf = (ROPE_THETA ** (-jnp.arange(0, MLA_ROPE, 2, dtype=F32) / MLA_ROPE)).reshape(-1, 1)

    x2 = x.reshape(t_tok, d_model)
    pos = positions.reshape(1, t_tok).astype(F32)
    tm = min(512, s_len)
    qig, fgate, q_all, k_all, v = _input_stage(
        x2, pos, invf, row(g_mix_norm[0]), w_qig, w_f, w_mla, row(g_cq[0]), w_qup, row(g_ckv[0]),
        w_kvup, g_qn, g_qr, g_kn, g_kr, tm)

    a = _hgrn_stage(qig, fgate, lb_param.astype(F32), g_hgrn_out[0].reshape(1, HG_W).astype(F32),
                    batch, s_len)
    ob = _attn_stage(q_all, k_all, v, batch, s_len, min(512, s_len))

    wo = w_out[0].astype(BF16)
    out = _output_stage(x2, a, ob, row(g_mla_out[0]), wo[:HG_W], wo[HG_W:], row(g_ffn_norm[0]),
                        w_up[0].astype(BF16), w_down[0].astype(BF16), tm)
    return out.reshape(batch, s_len, d_model)
```

```python
import functools

import jax
import jax.numpy as jnp
from jax import lax
from jax.experimental import pallas as pl
from jax.experimental.pallas import tpu as pltpu

NORM_EPS = 1e-6
HG_HEADS = 4
HG_D = 128
HG_W = HG_HEADS * HG_D
MLA_HEADS = 4
MLA_NOPE = 128
MLA_ROPE = 64
MLA_V = 128
MLA_QK = MLA_NOPE + MLA_ROPE
Q_LORA = 384
KV_LORA = 256
ROPE_THETA = 10000.0
LANES = 128
QK_PAD = 2 * LANES

HG_CHUNK = 128
HG_LEAF = 8
HG_UNROLL = 2
VMEM_LIMIT = 56 * 1024 * 1024

F32 = jnp.float32
BF16 = jnp.bfloat16


def _dot(a, b):
    return jnp.dot(a, b, preferred_element_type=F32)


def _dot_nt(a, b):
    return lax.dot_general(a, b, (((1,), (1,)), ((), ())), preferred_element_type=F32)


def _rms(t, gain):
    return t * lax.rsqrt(jnp.mean(t * t, axis=-1, keepdims=True) + NORM_EPS) * gain


def _sigmoid(t):
    return 0.5 * jnp.tanh(0.5 * t) + 0.5


def _swap_rope_halves(t):
    lane = lax.broadcasted_iota(jnp.int32, t.shape, 1)
    return jnp.where(lane < MLA_ROPE // 2, pltpu.roll(t, LANES - MLA_ROPE // 2, 1),
                     pltpu.roll(t, MLA_ROPE // 2, 1))


def _input_kernel(x_ref, pos_ref, invf_ref, g_mix_ref, w_qig_ref, w_f_ref, w_mla_ref, g_cq_ref,
                  w_qup_ref, g_ckv_ref, w_kvup_ref, g_qn_ref, g_qr_ref, g_kn_ref, g_kr_ref,
                  qig_ref, f_ref, qa_ref, ka_ref, v_ref):
    x = x_ref[...]
    h = _rms(x, g_mix_ref[...]).astype(BF16)
    qig_ref[...] = _dot(h, w_qig_ref[...]).astype(BF16)
    f_ref[...] = _dot(h, w_f_ref[...])
    m = _dot(h, w_mla_ref[...])
    c_q = m[:, :Q_LORA]
    c_kv = m[:, Q_LORA:Q_LORA + KV_LORA]
    k_r = m[:, Q_LORA + KV_LORA:]
    q = _dot(_rms(c_q, g_cq_ref[...]).astype(BF16), w_qup_ref[...])
    kv = _dot(_rms(c_kv, g_ckv_ref[...]).astype(BF16), w_kvup_ref[...])
    v_ref[...] = kv[:, MLA_HEADS * MLA_NOPE:].astype(BF16)

    ang = invf_ref[...] * pos_ref[...]
    c, s = jnp.cos(ang), jnp.sin(ang)
    z = jnp.zeros_like(c)
    cos_t = jnp.concatenate([c, c, z, z], axis=0).T
    sin_t = jnp.concatenate([-s, s, z, z], axis=0).T

    def rope(t):
        return t * cos_t + _swap_rope_halves(t) * sin_t

    k_pe = rope(k_r * g_kr_ref[...])
    ssq_kr = jnp.sum(k_r * k_r, axis=-1, keepdims=True)
    scale = MLA_QK ** -0.5
    for hd in range(MLA_HEADS):
        lo, hi = hd * LANES, (hd + 1) * LANES
        q_n = q[:, lo:hi]
        q_r = q[:, MLA_HEADS * MLA_NOPE + lo:MLA_HEADS * MLA_NOPE + hi]
        ssq = jnp.sum(q_n * q_n, axis=-1, keepdims=True) + jnp.sum(q_r * q_r, axis=-1, keepdims=True)
        rstd = lax.rsqrt(ssq * (1.0 / MLA_QK) + NORM_EPS) * scale
        q_pe = rope(q_r * g_qr_ref[...])
        qa_ref[hd] = jnp.concatenate([q_n * g_qn_ref[...] * rstd, q_pe * rstd], axis=1).astype(BF16)
        k_n = kv[:, lo:hi]
        ssq = jnp.sum(k_n * k_n, axis=-1, keepdims=True) + ssq_kr
        rstd = lax.rsqrt(ssq * (1.0 / MLA_QK) + NORM_EPS)
        ka_ref[hd] = jnp.concatenate([k_n * g_kn_ref[...] * rstd, k_pe * rstd], axis=1).astype(BF16)


def _const_spec(shape):
    nd = len(shape)
    return pl.BlockSpec(shape, lambda *_: (0,) * nd, pipeline_mode=pl.Buffered(1))


def _input_stage(x2, pos, invf, g_mix, w_qig, w_f, w_mla, g_cq, w_qup, g_ckv, w_kvup,
                 g_qn, g_qr, g_kn, g_kr, tm):
    t_tok, d_model = x2.shape
    consts = (invf, g_mix, w_qig, w_f, w_mla, g_cq, w_qup, g_ckv, w_kvup, g_qn, g_qr, g_kn, g_kr)
    return pl.pallas_call(
        _input_kernel,
        grid=(t_tok // tm,),
        in_specs=[pl.BlockSpec((tm, d_model), lambda i: (i, 0)),
                  pl.BlockSpec((1, tm), lambda i: (0, i))] + [_const_spec(c.shape) for c in consts],
        out_specs=[pl.BlockSpec((tm, 3 * HG_W), lambda i: (i, 0)),
                   pl.BlockSpec((tm, 2 * HG_W), lambda i: (i, 0)),
                   pl.BlockSpec((MLA_HEADS, tm, QK_PAD), lambda i: (0, i, 0)),
                   pl.BlockSpec((MLA_HEADS, tm, QK_PAD), lambda i: (0, i, 0)),
                   pl.BlockSpec((tm, MLA_HEADS * MLA_V), lambda i: (i, 0))],
        out_shape=[jax.ShapeDtypeStruct((t_tok, 3 * HG_W), BF16),
                   jax.ShapeDtypeStruct((t_tok, 2 * HG_W), F32),
                   jax.ShapeDtypeStruct((MLA_HEADS, t_tok, QK_PAD), BF16),
                   jax.ShapeDtypeStruct((MLA_HEADS, t_tok, QK_PAD), BF16),
                   jax.ShapeDtypeStruct((t_tok, MLA_HEADS * MLA_V), BF16)],
        compiler_params=pltpu.CompilerParams(dimension_semantics=("arbitrary",),
                                             vmem_limit_bytes=VMEM_LIMIT),
        name="input_stage",
    )(x2, pos, *consts)


def _hg_levels():
    sizes = []
    half = HG_CHUNK // 2
    while half >= HG_LEAF:
        sizes.append(half)
        half //= 2
    return tuple(sizes)


def _hg_chunks(jobs, lvl_masks, ones):
    c = HG_CHUNK
    for j in jobs:
        half_gap = 0.5 * (1.0 - j["lb"])
        th = jnp.tanh(0.5 * j["logit"])
        f = (j["lb"] + half_gap) + half_gap * th
        j["k"] = half_gap - half_gap * th
        g = jnp.log2(f)
        g_hi = g.astype(BF16)
        r1 = g - g_hi.astype(F32)
        g_mid = r1.astype(BF16)
        g_lo = (r1 - g_mid.astype(F32)).astype(BF16)
        j["b"] = _dot(j["tri"], jnp.concatenate([g_hi, g_mid, g_lo], axis=0))

    for j in jobs:
        q, k, b = j["q"], j["k"], j["b"]
        b_tot = b[0:1] if j["rev"] else b[c - 1:c]
        j["dec"] = jnp.exp2(b_tot)
        j["q_in"] = (q * jnp.exp2(b)).astype(BF16)
        k_out = (k * jnp.exp2(b_tot - b)).astype(BF16)
        j["st_add"] = _dot(j["v"].T.astype(BF16), k_out)
        j["kpad"][HG_LEAF:HG_LEAF + c, :] = k
        j["bpad"][HG_LEAF:HG_LEAF + c, :] = b
        j["p"] = _dot((q * k).astype(BF16), ones) * j["leaf_masks"][0]

    for lvl, half in enumerate(_hg_levels()):
        n_grp = c // (2 * half)
        shape3 = (n_grp, 2 * half, HG_D)
        zero = jnp.zeros((n_grp, half, HG_D), F32)
        for j in jobs:
            b3, q3, k3 = j["b"].reshape(shape3), j["q"].reshape(shape3), j["k"].reshape(shape3)
            if j["rev"]:
                mid = b3[:, half:half + 1]
                q_l = jnp.concatenate([q3[:, :half] * jnp.exp2(b3[:, :half] - mid), zero], axis=1)
                k_l = jnp.concatenate([zero, k3[:, half:] * jnp.exp2(mid - b3[:, half:])], axis=1)
            else:
                mid = b3[:, half - 1:half]
                q_l = jnp.concatenate([zero, q3[:, half:] * jnp.exp2(b3[:, half:] - mid)], axis=1)
                k_l = jnp.concatenate([k3[:, :half] * jnp.exp2(mid - b3[:, :half]), zero], axis=1)
            p_l = _dot_nt(q_l.reshape(c, HG_D).astype(BF16), k_l.reshape(c, HG_D).astype(BF16))
            j["p"] = j["p"] + (p_l * lvl_masks[lvl] if n_grp > 1 else p_l)

    for d in range(1, HG_LEAF):
        for j in jobs:
            lo = HG_LEAF + d if j["rev"] else HG_LEAF - d
            k_s = j["kpad"][lo:lo + c, :]
            b_s = j["bpad"][lo:lo + c, :]
            w = j["q"] * k_s * jnp.exp2(j["b"] - b_s)
            j["p"] = j["p"] + _dot(w.astype(BF16), ones) * j["leaf_masks"][d]

    return [(_dot(j["p"].astype(BF16), j["v"].astype(BF16)), j["q_in"], j["st_add"], j["dec"])
            for j in jobs]


def _hgrn_kernel(q_ref, i_ref, g_ref, ff_ref, fb_ref, lbp_ref, gain_ref, tri_ref, lvl_ref, leaf_ref,
                 ones_ref, out_ref, o_ref, qin_ref, add_ref, dec_ref, kpad_ref, bpad_ref):
    s_len = q_ref.shape[0]
    n_chunk = s_len // HG_CHUNK
    logit_refs = (ff_ref, fb_ref)
    chunk_rows = lambda ci: pl.ds(pl.multiple_of(ci * HG_CHUNK, HG_CHUNK), HG_CHUNK)

    def lower_bound(d):
        p = lbp_ref[d]
        e = jnp.exp(p - jnp.max(p, axis=0, keepdims=True))
        return e[0:1] / jnp.sum(e, axis=0, keepdims=True)

    lbs = (lower_bound(0), lower_bound(1))
    kpad_ref[...] = jnp.zeros_like(kpad_ref)
    bpad_ref[...] = jnp.zeros_like(bpad_ref)

    def within_chunks(n, carry):
        lvl_masks = [lvl_ref[l] for l in range(lvl_ref.shape[0])]
        jobs = []
        for u in range(HG_UNROLL):
            ci = n * HG_UNROLL + u
            rows = chunk_rows(ci)
            for d, rev in enumerate((False, True)):
                slot = d * HG_UNROLL + u
                jobs.append(dict(
                    ci=ci, rows=rows, d=d, rev=rev, lb=lbs[d], tri=tri_ref[d],
                    q=q_ref[rows, :].astype(F32), logit=logit_refs[d][rows, :],
                    v=i_ref[rows, :].astype(F32), kpad=kpad_ref.at[slot], bpad=bpad_ref.at[slot],
                    leaf_masks=[leaf_ref[d, m] for m in range(HG_LEAF)]))
        for j, (o, q_in, st_add, dec) in zip(jobs, _hg_chunks(jobs, lvl_masks, ones_ref[...])):
            o_ref[j["d"], j["rows"], :] = o
            qin_ref[j["d"], j["rows"], :] = q_in
            add_ref[j["d"], j["ci"]] = st_add
            dec_ref[j["d"], j["ci"]] = dec
        return carry

    lax.fori_loop(0, n_chunk // HG_UNROLL, within_chunks, 0)

    def across_chunks(n, states):
        new_states = []
        for d, st in enumerate(states):
            ci = (n_chunk - 1 - n) if d else n
            rows = chunk_rows(ci)
            o_ref[d, rows, :] += _dot_nt(qin_ref[d, rows, :], st.astype(BF16))
            new_states.append(dec_ref[d, ci] * st + add_ref[d, ci])
        return tuple(new_states)

    zero_state = jnp.zeros((HG_D, HG_D), F32)
    lax.fori_loop(0, n_chunk, across_chunks, (zero_state, zero_state))

    def finish(n, carry):
        rows = chunk_rows(n)
        y = _rms(o_ref[0, rows, :] + o_ref[1, rows, :], gain_ref[...])
        gate = g_ref[rows, :].astype(F32)
        out_ref[rows, :] = (y * gate * _sigmoid(gate)).astype(BF16)
        return carry

    lax.fori_loop(0, n_chunk, finish, 0)


def _hgrn_stage(qig, fgate, lb_param, gain, batch, s_len):
    t_tok = qig.shape[0]
    c = HG_CHUNK
    r = jnp.arange(c)
    t_i, s_i = r[:, None], r[None, :]
    lower = s_i <= t_i
    tri = jnp.stack([jnp.tile(lower, (1, 3)), jnp.tile(lower.T, (1, 3))]).astype(BF16)
    lvl_masks = jnp.stack([t_i // (2 * h) == s_i // (2 * h) for h in _hg_levels()]).astype(F32)
    same_leaf = t_i // HG_LEAF == s_i // HG_LEAF
    leaf_masks = jnp.stack([jnp.stack([(s_i == t_i + sign * d) & same_leaf for d in range(HG_LEAF)])
                            for sign in (-1, 1)]).astype(F32)
    ones = jnp.ones((HG_D, c), BF16)
    blk = lambda col: pl.BlockSpec((s_len, HG_D), lambda b, h, col=col: (b, col * HG_HEADS + h))
    return pl.pallas_call(
        _hgrn_kernel,
        grid=(batch, HG_HEADS),
        in_specs=[blk(0), blk(1), blk(2), blk(0), blk(1),
                  pl.BlockSpec((2, 2, HG_D), lambda b, h: (0, 0, h)),
                  pl.BlockSpec((1, HG_D), lambda b, h: (0, h)),
                  _const_spec(tri.shape), _const_spec(lvl_masks.shape), _const_spec(leaf_masks.shape),
                  _const_spec(ones.shape)],
        out_specs=pl.BlockSpec((s_len, HG_D), lambda b, h: (b, h)),
        out_shape=jax.ShapeDtypeStruct((t_tok, HG_W), BF16),
        scratch_shapes=[pltpu.VMEM((2, s_len, HG_D), F32), pltpu.VMEM((2, s_len, HG_D), BF16),
                        pltpu.VMEM((2, s_len // c, HG_D, HG_D), F32),
                        pltpu.VMEM((2, s_len // c, 1, HG_D), F32),
                        pltpu.VMEM((2 * HG_UNROLL, c + 2 * HG_LEAF, HG_D), F32),
                        pltpu.VMEM((2 * HG_UNROLL, c + 2 * HG_LEAF, HG_D), F32)],
        compiler_params=pltpu.CompilerParams(dimension_semantics=("arbitrary", "arbitrary"),
                                             vmem_limit_bytes=VMEM_LIMIT),
        name="hgrn_stage",
    )(qig, qig, qig, fgate, fgate, lb_param, gain, tri, lvl_masks, leaf_masks, ones)


def _attn_kernel(q_ref, k_ref, v_ref, o_ref):
    s = _dot_nt(q_ref[0], k_ref[0])
    p = jnp.exp(s - jnp.max(s, axis=-1, keepdims=True))
    o = _dot(p.astype(BF16), v_ref[...])
    o_ref[...] = (o / jnp.sum(p, axis=-1, keepdims=True)).astype(BF16)


def _attn_stage(q_all, k_all, v, batch, s_len, tq):
    t_tok = v.shape[0]
    nq = s_len // tq
    return pl.pallas_call(
        _attn_kernel,
        grid=(batch, MLA_HEADS, nq),
        in_specs=[pl.BlockSpec((1, tq, QK_PAD), lambda b, h, i: (h, b * nq + i, 0)),
                  pl.BlockSpec((1, s_len, QK_PAD), lambda b, h, i: (h, b, 0)),
                  pl.BlockSpec((s_len, MLA_V), lambda b, h, i: (b, h))],
        out_specs=pl.BlockSpec((tq, MLA_V), lambda b, h, i: (b * nq + i, h)),
        out_shape=jax.ShapeDtypeStruct((t_tok, MLA_HEADS * MLA_V), BF16),
        compiler_params=pltpu.CompilerParams(dimension_semantics=("arbitrary",) * 3,
                                             vmem_limit_bytes=VMEM_LIMIT),
        name="attn_stage",
    )(q_all, k_all, v)


def _output_kernel(x_ref, a_ref, ob_ref, g_mla_ref, w_oa_ref, w_ob_ref, g_ffn_ref, w_up_ref,
                   w_dn_ref, out_ref):
    ob = _rms(ob_ref[...].astype(F32), g_mla_ref[...]).astype(BF16)
    x1 = x_ref[...] + _dot(a_ref[...], w_oa_ref[...]) + _dot(ob, w_ob_ref[...])
    h = _rms(x1, g_ffn_ref[...]).astype(BF16)
    up = jnp.maximum(_dot(h, w_up_ref[...]), 0.0)
    out_ref[...] = x1 + _dot((up * up).astype(BF16), w_dn_ref[...])


def _output_stage(x2, a, ob, g_mla, w_oa, w_ob, g_ffn, w_up, w_dn, tm):
    t_tok, d_model = x2.shape
    consts = (g_mla, w_oa, w_ob, g_ffn, w_up, w_dn)
    return pl.pallas_call(
        _output_kernel,
        grid=(t_tok // tm,),
        in_specs=[pl.BlockSpec((tm, d_model), lambda i: (i, 0)),
                  pl.BlockSpec((tm, HG_W), lambda i: (i, 0)),
                  pl.BlockSpec((tm, MLA_HEADS * MLA_V), lambda i: (i, 0))]
                 + [_const_spec(c.shape) for c in consts],
        out_specs=pl.BlockSpec((tm, d_model), lambda i: (i, 0)),
        out_shape=jax.ShapeDtypeStruct((t_tok, d_model), F32),
        compiler_params=pltpu.CompilerParams(dimension_semantics=("arbitrary",),
                                             vmem_limit_bytes=VMEM_LIMIT),
        name="output_stage",
    )(x2, a, ob, *consts)


def kernel(x, positions, g_mix_norm, w_in, lb_param, g_hgrn_out, g_cq, w_q_up, g_ckv, w_kv_up,
           g_q_norm, g_k_norm, g_mla_out, w_out, g_ffn_norm, w_up, w_down):
    batch, s_len, d_model = x.shape
    assert g_mix_norm.shape[0] == 1 and lb_param.shape[1] == 2, "one layer"
    assert s_len % (HG_CHUNK * HG_UNROLL) == 0
    t_tok = batch * s_len
    row = lambda t: t.reshape(1, -1).astype(F32)

    w = w_in[0]
    o_q, o_ff, o_fb, o_i, o_g, o_cq = (HG_W * n for n in range(6))
    o_ckv = o_cq + Q_LORA
    o_kr = o_ckv + KV_LORA
    w_qig = jnp.concatenate([w[:, o_q:o_ff], w[:, o_i:o_g], w[:, o_g:o_cq]], axis=1).astype(BF16)
    w_f = w[:, o_ff:o_i].astype(BF16)
    w_mla = jnp.pad(w[:, o_cq:], ((0, 0), (0, LANES - MLA_ROPE))).astype(BF16)

    wq = w_q_up[0].reshape(Q_LORA, MLA_HEADS, MLA_QK)
    wq_rope = jnp.pad(wq[:, :, MLA_NOPE:], ((0, 0), (0, 0), (0, LANES - MLA_ROPE)))
    w_qup = jnp.concatenate([wq[:, :, :MLA_NOPE].reshape(Q_LORA, -1),
                             wq_rope.reshape(Q_LORA, -1)], axis=1).astype(BF16)
    wkv = w_kv_up[0].reshape(KV_LORA, MLA_HEADS, MLA_NOPE + MLA_V)
    w_kvup = jnp.concatenate([wkv[:, :, :MLA_NOPE].reshape(KV_LORA, -1),
                              wkv[:, :, MLA_NOPE:].reshape(KV_LORA, -1)], axis=1).astype(BF16)
    pad_rope = lambda t: jnp.pad(t, (0, LANES - MLA_ROPE)).reshape(1, LANES).astype(F32)
    g_qn, g_qr = row(g_q_norm[0, :MLA_NOPE]), pad_rope(g_q_norm[0, MLA_NOPE:])
    g_kn, g_kr = row(g_k_norm[0, :MLA_NOPE]), pad_rope(g_k_norm[0, MLA_NOPE:])
    invf = (ROPE_THETA ** (-jnp.arange(0, MLA_ROPE, 2, dtype=F32) / MLA_ROPE)).reshape(-1, 1)

    x2 = x.reshape(t_tok, d_model)
    pos = positions.reshape(1, t_tok).astype(F32)
    tm = min(512, s_len)
    qig, fgate, q_all, k_all, v = _input_stage(
        x2, pos, invf, row(g_mix_norm[0]), w_qig, w_f, w_mla, row(g_cq[0]), w_qup, row(g_ckv[0]),
        w_kvup, g_qn, g_qr, g_kn, g_kr, tm)

    a = _hgrn_stage(qig, fgate, lb_param.astype(F32), g_hgrn_out[0].reshape(1, HG_W).astype(F32),
                    batch, s_len)
    ob = _attn_stage(q_all, k_all, v, batch, s_len, min(512, s_len))

    wo = w_out[0].astype(BF16)
    out = _output_stage(x2, a, ob, row(g_mla_out[0]), wo[:HG_W], wo[HG_W:], row(g_ffn_norm[0]),
                        w_up[0].astype(BF16), w_down[0].astype(BF16), tm)
    return out.reshape(batch, s_len, d_model)
```

```python
import functools

import jax
import jax.numpy as jnp
from jax import lax
from jax.experimental import pallas as pl
from jax.experimental.pallas import tpu as pltpu

NORM_EPS = 1e-6
HG_HEADS = 4
HG_D = 128
HG_W = HG_HEADS * HG_D
MLA_HEADS = 4
MLA_NOPE = 128
MLA_ROPE = 64
MLA_V = 128
MLA_QK = MLA_NOPE + MLA_ROPE
Q_LORA = 384
KV_LORA = 256
ROPE_THETA = 10000.0
LANES = 128
QK_PAD = 2 * LANES

HG_CHUNK = 128
HG_LEAF = 8
HG_UNROLL = 2
ATTN_SUB = 256
INPUT_SUB = 256
LOG2_E = 1.4426950408889634
VMEM_LIMIT = 56 * 1024 * 1024

F32 = jnp.float32
BF16 = jnp.bfloat16


def _dot(a, b):
    return jnp.dot(a, b, preferred_element_type=F32)


def _dot_nt(a, b):
    return lax.dot_general(a, b, (((1,), (1,)), ((), ())), preferred_element_type=F32)


def _rms(t, gain):
    return t * lax.rsqrt(jnp.mean(t * t, axis=-1, keepdims=True) + NORM_EPS) * gain


def _sigmoid(t):
    return 0.5 * jnp.tanh(0.5 * t) + 0.5


def _input_kernel(x_ref, pos_ref, invf_ref, g_mix_ref, w_qig_ref, w_f_ref, w_mla_ref, g_cq_ref,
                  w_qup_ref, g_ckv_ref, w_kvup_ref, g_qn_ref, g_qr_ref, g_kn_ref, g_kr_ref,
                  qig_ref, f_ref, qa_ref, ka_ref, v_ref):
    tm = x_ref.shape[0]
    sub = min(INPUT_SUB, tm)
    tiles = [dict(rows=slice(i * sub, (i + 1) * sub)) for i in range(tm // sub)]
    scale = MLA_QK ** -0.5 * LOG2_E

    def project(t):
        h = _rms(x_ref[t["rows"], :], g_mix_ref[...]).astype(BF16)
        qig_ref[t["rows"], :] = _dot(h, w_qig_ref[...]).astype(BF16)
        f_ref[t["rows"], :] = _dot(h, w_f_ref[...])
        t["m"] = _dot(h, w_mla_ref[...])

    def up_project(t):
        c_q = t["m"][:, :Q_LORA]
        c_kv = t["m"][:, Q_LORA:Q_LORA + KV_LORA]
        t["q"] = _dot(_rms(c_q, g_cq_ref[...]).astype(BF16), w_qup_ref[...])
        t["kv"] = _dot(_rms(c_kv, g_ckv_ref[...]).astype(BF16), w_kvup_ref[...])

    def rope_tables():
        ang = invf_ref[...] * pos_ref[...]
        c, s = jnp.cos(ang), jnp.sin(ang)
        z = jnp.zeros_like(c)
        return jnp.concatenate([c, c, z, z], axis=0).T, jnp.concatenate([-s, s, z, z], axis=0).T

    def heads(t, cos_t, sin_t):
        rows, q, kv = t["rows"], t["q"], t["kv"]
        v_ref[rows, :] = kv[:, MLA_HEADS * MLA_NOPE:].astype(BF16)
        cos_r, sin_r = cos_t[rows, :], sin_t[rows, :]
        q_cos, q_sin = g_qr_ref[0:1] * cos_r, g_qr_ref[1:2] * sin_r
        rope_cols = Q_LORA + KV_LORA
        k_r = t["m"][:, rope_cols:rope_cols + LANES]
        k_pe = (k_r * (g_kr_ref[0:1] * cos_r)
                + t["m"][:, rope_cols + LANES:] * (g_kr_ref[1:2] * sin_r))
        sq_kr = k_r * k_r
        for hd in range(MLA_HEADS):
            lo, hi = hd * LANES, (hd + 1) * LANES
            q_n = q[:, lo:hi]
            q_r = q[:, MLA_HEADS * MLA_NOPE + lo:MLA_HEADS * MLA_NOPE + hi]
            q_r_swapped = q[:, 2 * MLA_HEADS * MLA_NOPE + lo:2 * MLA_HEADS * MLA_NOPE + hi]
            ssq = jnp.sum(q_n * q_n + q_r * q_r, axis=-1, keepdims=True)
            rstd = lax.rsqrt(ssq * (1.0 / MLA_QK) + NORM_EPS) * scale
            q_pe = q_r * q_cos + q_r_swapped * q_sin
            qa_ref[hd, rows, :] = jnp.concatenate([q_n * g_qn_ref[...] * rstd, q_pe * rstd],
                                                  axis=1).astype(BF16)
            k_n = kv[:, lo:hi]
            ssq = jnp.sum(k_n * k_n + sq_kr, axis=-1, keepdims=True)
            rstd = lax.rsqrt(ssq * (1.0 / MLA_QK) + NORM_EPS)
            ka_ref[hd, rows, :] = jnp.concatenate([k_n * g_kn_ref[...] * rstd, k_pe * rstd],
                                                  axis=1).astype(BF16)

    tables = None
    for i, t in enumerate(tiles):
        project(t)
        if i == 0:
            tables = rope_tables()
        else:
            heads(tiles[i - 1], *tables)
        up_project(t)
    heads(tiles[-1], *tables)


def _const_spec(shape):
    nd = len(shape)
    return pl.BlockSpec(shape, lambda *_: (0,) * nd, pipeline_mode=pl.Buffered(1))


def _input_stage(x2, pos, invf, g_mix, w_qig, w_f, w_mla, g_cq, w_qup, g_ckv, w_kvup,
                 g_qn, g_qr, g_kn, g_kr, tm):
    t_tok, d_model = x2.shape
    consts = (invf, g_mix, w_qig, w_f, w_mla, g_cq, w_qup, g_ckv, w_kvup, g_qn, g_qr, g_kn, g_kr)
    return pl.pallas_call(
        _input_kernel,
        grid=(t_tok // tm,),
        in_specs=[pl.BlockSpec((tm, d_model), lambda i: (i, 0)),
                  pl.BlockSpec((1, tm), lambda i: (0, i))] + [_const_spec(c.shape) for c in consts],
        out_specs=[pl.BlockSpec((tm, 3 * HG_W), lambda i: (i, 0)),
                   pl.BlockSpec((tm, 2 * HG_W), lambda i: (i, 0)),
                   pl.BlockSpec((MLA_HEADS, tm, QK_PAD), lambda i: (0, i, 0)),
                   pl.BlockSpec((MLA_HEADS, tm, QK_PAD), lambda i: (0, i, 0)),
                   pl.BlockSpec((tm, MLA_HEADS * MLA_V), lambda i: (i, 0))],
        out_shape=[jax.ShapeDtypeStruct((t_tok, 3 * HG_W), BF16),
                   jax.ShapeDtypeStruct((t_tok, 2 * HG_W), F32),
                   jax.ShapeDtypeStruct((MLA_HEADS, t_tok, QK_PAD), BF16),
                   jax.ShapeDtypeStruct((MLA_HEADS, t_tok, QK_PAD), BF16),
                   jax.ShapeDtypeStruct((t_tok, MLA_HEADS * MLA_V), BF16)],
        compiler_params=pltpu.CompilerParams(dimension_semantics=("arbitrary",),
                                             vmem_limit_bytes=VMEM_LIMIT),
        name="input_stage",
    )(x2, pos, *consts)


def _hg_levels():
    sizes = []
    half = HG_CHUNK // 2
    while half >= HG_LEAF:
        sizes.append(half)
        half //= 2
    return tuple(sizes)


def _hg_chunks(jobs, lvl_masks):
    c = HG_CHUNK
    for j in jobs:
        half_gap = 0.5 * (1.0 - j["lb"])
        th = jnp.tanh(0.5 * j["logit"])
        f = (j["lb"] + half_gap) + half_gap * th
        j["k"] = half_gap - half_gap * th
        g = jnp.log2(f)
        g_hi = g.astype(BF16)
        r1 = g - g_hi.astype(F32)
        g_mid = r1.astype(BF16)
        g_lo = (r1 - g_mid.astype(F32)).astype(BF16)
        j["b"] = _dot(j["tri"], jnp.concatenate([g_hi, g_mid, g_lo], axis=0))

    for j in jobs:
        q, k, b = j["q"], j["k"], j["b"]
        b_tot = b[0:1] if j["rev"] else b[c - 1:c]
        j["dec"] = jnp.exp2(b_tot)
        j["q_in"] = (q * jnp.exp2(b)).astype(BF16)
        k_out = (k * jnp.exp2(b_tot - b)).astype(BF16)
        j["st_add"] = _dot(j["v"].T.astype(BF16), k_out)
        j["kpad"][HG_LEAF:HG_LEAF + c, :] = k
        j["bpad"][HG_LEAF:HG_LEAF + c, :] = b
        j["p"] = None

    for lvl, half in enumerate(_hg_levels()):
        n_grp = c // (2 * half)
        shape3 = (n_grp, 2 * half, HG_D)
        zero = jnp.zeros((n_grp, half, HG_D), F32)
        for j in jobs:
            b3, q3, k3 = j["b"].reshape(shape3), j["q"].reshape(shape3), j["k"].reshape(shape3)
            if j["rev"]:
                mid = b3[:, half:half + 1]
                q_l = jnp.concatenate([q3[:, :half] * jnp.exp2(b3[:, :half] - mid), zero], axis=1)
                k_l = jnp.concatenate([zero, k3[:, half:] * jnp.exp2(mid - b3[:, half:])], axis=1)
            else:
                mid = b3[:, half - 1:half]
                q_l = jnp.concatenate([zero, q3[:, half:] * jnp.exp2(b3[:, half:] - mid)], axis=1)
                k_l = jnp.concatenate([k3[:, :half] * jnp.exp2(mid - b3[:, :half]), zero], axis=1)
            p_l = _dot_nt(q_l.reshape(c, HG_D).astype(BF16), k_l.reshape(c, HG_D).astype(BF16))
            p_l = p_l * lvl_masks[lvl] if n_grp > 1 else p_l
            j["p"] = p_l if j["p"] is None else j["p"] + p_l

    for j in jobs:
        w = [(j["q"] * j["k"]).astype(BF16)]
        for d in range(1, HG_LEAF):
            lo = HG_LEAF + d if j["rev"] else HG_LEAF - d
            k_s = j["kpad"][lo:lo + c, :]
            b_s = j["bpad"][lo:lo + c, :]
            w.append((j["q"] * k_s * jnp.exp2(j["b"] - b_s)).astype(BF16))
        j["row_sums"] = _dot(jnp.concatenate(w, axis=1), j["spread"])

    outs = []
    for j in jobs:
        p_leaf = pltpu.roll(j["row_sums"], 0, 1, stride=1, stride_axis=0) * j["leaf_mask"]
        p = (j["p"] + p_leaf).astype(BF16)
        outs.append((_dot(p, j["v"].astype(BF16)), j["q_in"], j["st_add"], j["dec"]))
    return outs


def _hgrn_kernel(q_ref, i_ref, g_ref, ff_ref, fb_ref, lbp_ref, gain_ref, tri_ref, lvl_ref, leaf_ref,
                 spread_ref, out_ref, o_ref, qin_ref, add_ref, dec_ref, kpad_ref, bpad_ref):
    s_len = q_ref.shape[0]
    n_chunk = s_len // HG_CHUNK
    logit_refs = (ff_ref, fb_ref)
    chunk_rows = lambda ci: pl.ds(pl.multiple_of(ci * HG_CHUNK, HG_CHUNK), HG_CHUNK)

    def lower_bound(d):
        p = lbp_ref[d]
        e = jnp.exp(p - jnp.max(p, axis=0, keepdims=True))
        return e[0:1] / jnp.sum(e, axis=0, keepdims=True)

    lbs = (lower_bound(0), lower_bound(1))
    kpad_ref[...] = jnp.zeros_like(kpad_ref)
    bpad_ref[...] = jnp.zeros_like(bpad_ref)

    def within_chunks(n, carry):
        lvl_masks = [lvl_ref[l] for l in range(lvl_ref.shape[0])]
        jobs = []
        for u in range(HG_UNROLL):
            ci = n * HG_UNROLL + u
            rows = chunk_rows(ci)
            for d, rev in enumerate((False, True)):
                slot = d * HG_UNROLL + u
                jobs.append(dict(
                    ci=ci, rows=rows, d=d, rev=rev, lb=lbs[d], tri=tri_ref[d],
                    q=q_ref[rows, :].astype(F32), logit=logit_refs[d][rows, :],
                    v=i_ref[rows, :].astype(F32), kpad=kpad_ref.at[slot], bpad=bpad_ref.at[slot],
                    leaf_mask=leaf_ref[d], spread=spread_ref[d]))
        for j, (o, q_in, st_add, dec) in zip(jobs, _hg_chunks(jobs, lvl_masks)):
            o_ref[j["d"], j["rows"], :] = o
            qin_ref[j["d"], j["rows"], :] = q_in
            add_ref[j["d"], j["ci"]] = st_add
            dec_ref[j["d"], j["ci"]] = dec
        return carry

    lax.fori_loop(0, n_chunk // HG_UNROLL, within_chunks, 0)

    def across_chunks(n, states):
        new_states = []
        for d, st in enumerate(states):
            ci = (n_chunk - 1 - n) if d else n
            rows = chunk_rows(ci)
            o_ref[d, rows, :] += _dot_nt(qin_ref[d, rows, :], st.astype(BF16))
            new_states.append(dec_ref[d, ci] * st + add_ref[d, ci])
        return tuple(new_states)

    zero_state = jnp.zeros((HG_D, HG_D), F32)
    lax.fori_loop(0, n_chunk, across_chunks, (zero_state, zero_state))

    def finish(n, carry):
        rows = chunk_rows(n)
        y = _rms(o_ref[0, rows, :] + o_ref[1, rows, :], gain_ref[...])
        gate = g_ref[rows, :].astype(F32)
        out_ref[rows, :] = (y * gate * _sigmoid(gate)).astype(BF16)
        return carry

    lax.fori_loop(0, n_chunk, finish, 0)


def _hgrn_stage(qig, fgate, lb_param, gain, batch, s_len):
    assert HG_CHUNK == LANES, "the leaf path rotates each row of a (C, C) tile by its row index"
    t_tok = qig.shape[0]
    c = HG_CHUNK
    r = jnp.arange(c)
    t_i, s_i = r[:, None], r[None, :]
    lower = s_i <= t_i
    tri = jnp.stack([jnp.tile(lower, (1, 3)), jnp.tile(lower.T, (1, 3))]).astype(BF16)
    lvl_masks = jnp.stack([t_i // (2 * h) == s_i // (2 * h) for h in _hg_levels()]).astype(F32)
    same_leaf = t_i // HG_LEAF == s_i // HG_LEAF
    leaf_mask = jnp.stack([same_leaf & lower, same_leaf & lower.T]).astype(F32)
    dist = jnp.repeat(jnp.arange(HG_LEAF), HG_D)[:, None]
    spread = jnp.stack([(-s_i) % HG_LEAF == dist, s_i % HG_LEAF == dist]).astype(BF16)
    blk = lambda col: pl.BlockSpec((s_len, HG_D), lambda b, h, col=col: (b, col * HG_HEADS + h))
    return pl.pallas_call(
        _hgrn_kernel,
        grid=(batch, HG_HEADS),
        in_specs=[blk(0), blk(1), blk(2), blk(0), blk(1),
                  pl.BlockSpec((2, 2, HG_D), lambda b, h: (0, 0, h)),
                  pl.BlockSpec((1, HG_D), lambda b, h: (0, h)),
                  _const_spec(tri.shape), _const_spec(lvl_masks.shape), _const_spec(leaf_mask.shape),
                  _const_spec(spread.shape)],
        out_specs=pl.BlockSpec((s_len, HG_D), lambda b, h: (b, h)),
        out_shape=jax.ShapeDtypeStruct((t_tok, HG_W), BF16),
        scratch_shapes=[pltpu.VMEM((2, s_len, HG_D), F32), pltpu.VMEM((2, s_len, HG_D), BF16),
                        pltpu.VMEM((2, s_len // c, HG_D, HG_D), F32),
                        pltpu.VMEM((2, s_len // c, 1, HG_D), F32),
                        pltpu.VMEM((2 * HG_UNROLL, c + 2 * HG_LEAF, HG_D), F32),
                        pltpu.VMEM((2 * HG_UNROLL, c + 2 * HG_LEAF, HG_D), F32)],
        compiler_params=pltpu.CompilerParams(dimension_semantics=("arbitrary", "arbitrary"),
                                             vmem_limit_bytes=VMEM_LIMIT),
        name="hgrn_stage",
    )(qig, qig, qig, fgate, fgate, lb_param, gain, tri, lvl_masks, leaf_mask, spread)


def _attn_kernel(q_ref, k_ref, v_ref, o_ref):
    sub = min(ATTN_SUB, q_ref.shape[1])
    n_sub = q_ref.shape[1] // sub
    rows = lambda i: slice(i * sub, (i + 1) * sub)
    k = k_ref[0]
    v_t = v_ref[...].astype(F32).T.astype(BF16)
    scores_t = lambda i: _dot_nt(k, q_ref[0, rows(i), :])
    s_next = scores_t(0)
    for i in range(n_sub):
        s = s_next
        if i + 1 < n_sub:
            s_next = scores_t(i + 1)
        p = jnp.exp2(s - jnp.max(s, axis=0, keepdims=True))
        o_t = _dot(v_t, p.astype(BF16)) * (1.0 / jnp.sum(p, axis=0, keepdims=True))
        o_ref[rows(i), :] = o_t.T.astype(BF16)


def _attn_stage(q_all, k_all, v, batch, s_len, tq):
    t_tok = v.shape[0]
    nq = s_len // tq
    return pl.pallas_call(
        _attn_kernel,
        grid=(batch, MLA_HEADS, nq),
        in_specs=[pl.BlockSpec((1, tq, QK_PAD), lambda b, h, i: (h, b * nq + i, 0)),
                  pl.BlockSpec((1, s_len, QK_PAD), lambda b, h, i: (h, b, 0)),
                  pl.BlockSpec((s_len, MLA_V), lambda b, h, i: (b, h))],
        out_specs=pl.BlockSpec((tq, MLA_V), lambda b, h, i: (b * nq + i, h)),
        out_shape=jax.ShapeDtypeStruct((t_tok, MLA_HEADS * MLA_V), BF16),
        compiler_params=pltpu.CompilerParams(dimension_semantics=("arbitrary",) * 3,
                                             vmem_limit_bytes=VMEM_LIMIT),
        name="attn_stage",
    )(q_all, k_all, v)


def _output_kernel(x_ref, a_ref, ob_ref, g_mla_ref, w_oa_ref, w_ob_ref, g_ffn_ref, w_up_ref,
                   w_dn_ref, out_ref):
    ob = _rms(ob_ref[...].astype(F32), g_mla_ref[...]).astype(BF16)
    x1 = x_ref[...] + _dot(a_ref[...], w_oa_ref[...]) + _dot(ob, w_ob_ref[...])
    h = _rms(x1, g_ffn_ref[...]).astype(BF16)
    up = jnp.maximum(_dot(h, w_up_ref[...]), 0.0)
    out_ref[...] = x1 + _dot((up * up).astype(BF16), w_dn_ref[...])


def _output_stage(x2, a, ob, g_mla, w_oa, w_ob, g_ffn, w_up, w_dn, tm):
    t_tok, d_model = x2.shape
    consts = (g_mla, w_oa, w_ob, g_ffn, w_up, w_dn)
    return pl.pallas_call(
        _output_kernel,
        grid=(t_tok // tm,),
        in_specs=[pl.BlockSpec((tm, d_model), lambda i: (i, 0)),
                  pl.BlockSpec((tm, HG_W), lambda i: (i, 0)),
                  pl.BlockSpec((tm, MLA_HEADS * MLA_V), lambda i: (i, 0))]
                 + [_const_spec(c.shape) for c in consts],
        out_specs=pl.BlockSpec((tm, d_model), lambda i: (i, 0)),
        out_shape=jax.ShapeDtypeStruct((t_tok, d_model), F32),
        compiler_params=pltpu.CompilerParams(dimension_semantics=("arbitrary",),
                                             vmem_limit_bytes=VMEM_LIMIT),
        name="output_stage",
    )(x2, a, ob, *consts)


def kernel(x, positions, g_mix_norm, w_in, lb_param, g_hgrn_out, g_cq, w_q_up, g_ckv, w_kv_up,
           g_q_norm, g_k_norm, g_mla_out, w_out, g_ffn_norm, w_up, w_down):
    batch, s_len, d_model = x.shape
    assert g_mix_norm.shape[0] == 1 and lb_param.shape[1] == 2, "one layer"
    assert s_len % (HG_CHUNK * HG_UNROLL) == 0
    t_tok = batch * s_len
    row = lambda t: t.reshape(1, -1).astype(F32)

    w = w_in[0]
    o_q, o_ff, o_fb, o_i, o_g, o_cq = (HG_W * n for n in range(6))
    o_ckv = o_cq + Q_LORA
    o_kr = o_ckv + KV_LORA
    w_qig = jnp.concatenate([w[:, o_q:o_ff], w[:, o_i:o_g], w[:, o_g:o_cq]], axis=1).astype(BF16)
    w_f = w[:, o_ff:o_i].astype(BF16)
    half = MLA_ROPE // 2
    swap = lambda t: jnp.concatenate([t[..., half:], t[..., :half]], axis=-1)
    pad_lanes = lambda t: jnp.pad(t, [(0, 0)] * (t.ndim - 1) + [(0, LANES - MLA_ROPE)])
    w_kr = w[:, o_kr:]
    w_mla = jnp.concatenate([w[:, o_cq:o_kr], pad_lanes(w_kr), pad_lanes(swap(w_kr))],
                            axis=1).astype(BF16)

    wq = w_q_up[0].reshape(Q_LORA, MLA_HEADS, MLA_QK)
    wq_rope = wq[:, :, MLA_NOPE:]
    w_qup = jnp.concatenate([wq[:, :, :MLA_NOPE].reshape(Q_LORA, -1),
                             pad_lanes(wq_rope).reshape(Q_LORA, -1),
                             pad_lanes(swap(wq_rope)).reshape(Q_LORA, -1)], axis=1).astype(BF16)
    wkv = w_kv_up[0].reshape(KV_LORA, MLA_HEADS, MLA_NOPE + MLA_V)
    w_kvup = jnp.concatenate([wkv[:, :, :MLA_NOPE].reshape(KV_LORA, -1),
                              wkv[:, :, MLA_NOPE:].reshape(KV_LORA, -1)], axis=1).astype(BF16)
    rope_gain = lambda t: jnp.stack([pad_lanes(t), pad_lanes(swap(t))]).astype(F32)
    g_qn, g_qr = row(g_q_norm[0, :MLA_NOPE]), rope_gain(g_q_norm[0, MLA_NOPE:])
    g_kn, g_kr = row(g_k_norm[0, :MLA_NOPE]), rope_gain(g_k_norm[0, MLA_NOPE:])
    invf = (ROPE_THETA ** (-jnp.arange(0, MLA_ROPE, 2, dtype=F32) / MLA_ROPE)).reshape(-1, 1)

    x2 = x.reshape(t_tok, d_model)
    pos = positions.reshape(1, t_tok).astype(F32)
    tm = min(512, s_len)
    qig, fgate, q_all, k_all, v = _input_stage(
        x2, pos, invf, row(g_mix_norm[0]), w_qig, w_f, w_mla, row(g_cq[0]), w_qup, row(g_ckv[0]),
        w_kvup, g_qn, g_qr, g_kn, g_kr, tm)

    a = _hgrn_stage(qig, fgate, lb_param.astype(F32), g_hgrn_out[0].reshape(1, HG_W).astype(F32),
                    batch, s_len)
    ob = _attn_stage(q_all, k_all, v, batch, s_len, s_len)

    wo = w_out[0].astype(BF16)
    out = _output_stage(x2, a, ob, row(g_mla_out[0]), wo[:HG_W], wo[HG_W:], row(g_ffn_norm[0]),
                        w_up[0].astype(BF16), w_down[0].astype(BF16), tm)
    return out.reshape(batch, s_len, d_model)
```

```python
import functools

import jax
import jax.numpy as jnp
from jax import lax
from jax.experimental import pallas as pl
from jax.experimental.pallas import tpu as pltpu

NORM_EPS = 1e-6
HG_HEADS = 4
HG_D = 128
HG_W = HG_HEADS * HG_D
MLA_HEADS = 4
MLA_NOPE = 128
MLA_ROPE = 64
MLA_V = 128
MLA_QK = MLA_NOPE + MLA_ROPE
Q_LORA = 384
KV_LORA = 256
ROPE_THETA = 10000.0
LANES = 128
QK_PAD = 2 * LANES

HG_CHUNK = 128
HG_LEAF = 8
HG_UNROLL = 4
ATTN_SUB = 512
ATTN_DEPTH = 1
INPUT_SUB = 256
LOG2_E = 1.4426950408889634
VMEM_LIMIT = 56 * 1024 * 1024

F32 = jnp.float32
BF16 = jnp.bfloat16


def _dot(a, b):
    return jnp.dot(a, b, preferred_element_type=F32)


def _dot_nt(a, b):
    return lax.dot_general(a, b, (((1,), (1,)), ((), ())), preferred_element_type=F32)


def _rms(t, gain):
    return t * lax.rsqrt(jnp.mean(t * t, axis=-1, keepdims=True) + NORM_EPS) * gain


def _sigmoid(t):
    return 0.5 * jnp.tanh(0.5 * t) + 0.5


def _input_kernel(x_ref, pos_ref, invf_ref, g_mix_ref, w_qig_ref, w_f_ref, w_mla_ref, g_cq_ref,
                  w_qup_ref, g_ckv_ref, w_kvup_ref, g_qn_ref, g_qr_ref, g_kn_ref, g_kr_ref,
                  qig_ref, f_ref, qa_ref, ka_ref, v_ref):
    tm = x_ref.shape[0]
    sub = min(INPUT_SUB, tm)
    tiles = [dict(rows=slice(i * sub, (i + 1) * sub)) for i in range(tm // sub)]
    scale = MLA_QK ** -0.5 * LOG2_E

    def project(t):
        h = _rms(x_ref[t["rows"], :], g_mix_ref[...]).astype(BF16)
        qig_ref[t["rows"], :] = _dot(h, w_qig_ref[...]).astype(BF16)
        f_ref[t["rows"], :] = _dot(h, w_f_ref[...])
        t["m"] = _dot(h, w_mla_ref[...])

    def up_project(t):
        c_q = t["m"][:, :Q_LORA]
        c_kv = t["m"][:, Q_LORA:Q_LORA + KV_LORA]
        t["q"] = _dot(_rms(c_q, g_cq_ref[...]).astype(BF16), w_qup_ref[...])
        t["kv"] = _dot(_rms(c_kv, g_ckv_ref[...]).astype(BF16), w_kvup_ref[...])

    def rope_tables():
        ang = invf_ref[...] * pos_ref[...]
        c, s = jnp.cos(ang), jnp.sin(ang)
        z = jnp.zeros_like(c)
        return jnp.concatenate([c, c, z, z], axis=0).T, jnp.concatenate([-s, s, z, z], axis=0).T

    def heads(t, cos_t, sin_t):
        rows, q, kv = t["rows"], t["q"], t["kv"]
        v_ref[rows, :] = kv[:, MLA_HEADS * MLA_NOPE:].astype(BF16)
        cos_r, sin_r = cos_t[rows, :], sin_t[rows, :]
        q_cos, q_sin = g_qr_ref[0:1] * cos_r, g_qr_ref[1:2] * sin_r
        rope_cols = Q_LORA + KV_LORA
        k_r = t["m"][:, rope_cols:rope_cols + LANES]
        k_pe = (k_r * (g_kr_ref[0:1] * cos_r)
                + t["m"][:, rope_cols + LANES:] * (g_kr_ref[1:2] * sin_r))
        sq_kr = k_r * k_r
        for hd in range(MLA_HEADS):
            lo, hi = hd * LANES, (hd + 1) * LANES
            q_n = q[:, lo:hi]
            q_r = q[:, MLA_HEADS * MLA_NOPE + lo:MLA_HEADS * MLA_NOPE + hi]
            q_r_swapped = q[:, 2 * MLA_HEADS * MLA_NOPE + lo:2 * MLA_HEADS * MLA_NOPE + hi]
            ssq = jnp.sum(q_n * q_n + q_r * q_r, axis=-1, keepdims=True)
            rstd = lax.rsqrt(ssq * (1.0 / MLA_QK) + NORM_EPS) * scale
            q_pe = q_r * q_cos + q_r_swapped * q_sin
            qa_ref[hd, rows, :] = jnp.concatenate([q_n * g_qn_ref[...] * rstd, q_pe * rstd],
                                                  axis=1).astype(BF16)
            k_n = kv[:, lo:hi]
            ssq = jnp.sum(k_n * k_n + sq_kr, axis=-1, keepdims=True)
            rstd = lax.rsqrt(ssq * (1.0 / MLA_QK) + NORM_EPS)
            ka_ref[hd, rows, :] = jnp.concatenate([k_n * g_kn_ref[...] * rstd, k_pe * rstd],
                                                  axis=1).astype(BF16)

    tables = None
    for i, t in enumerate(tiles):
        project(t)
        if i == 0:
            tables = rope_tables()
        else:
            heads(tiles[i - 1], *tables)
        up_project(t)
    heads(tiles[-1], *tables)


def _const_spec(shape):
    nd = len(shape)
    return pl.BlockSpec(shape, lambda *_: (0,) * nd, pipeline_mode=pl.Buffered(1))


def _input_stage(x2, pos, invf, g_mix, w_qig, w_f, w_mla, g_cq, w_qup, g_ckv, w_kvup,
                 g_qn, g_qr, g_kn, g_kr, tm):
    t_tok, d_model = x2.shape
    consts = (invf, g_mix, w_qig, w_f, w_mla, g_cq, w_qup, g_ckv, w_kvup, g_qn, g_qr, g_kn, g_kr)
    return pl.pallas_call(
        _input_kernel,
        grid=(t_tok // tm,),
        in_specs=[pl.BlockSpec((tm, d_model), lambda i: (i, 0)),
                  pl.BlockSpec((1, tm), lambda i: (0, i))] + [_const_spec(c.shape) for c in consts],
        out_specs=[pl.BlockSpec((tm, 3 * HG_W), lambda i: (i, 0)),
                   pl.BlockSpec((tm, 2 * HG_W), lambda i: (i, 0)),
                   pl.BlockSpec((MLA_HEADS, tm, QK_PAD), lambda i: (0, i, 0)),
                   pl.BlockSpec((MLA_HEADS, tm, QK_PAD), lambda i: (0, i, 0)),
                   pl.BlockSpec((tm, MLA_HEADS * MLA_V), lambda i: (i, 0))],
        out_shape=[jax.ShapeDtypeStruct((t_tok, 3 * HG_W), BF16),
                   jax.ShapeDtypeStruct((t_tok, 2 * HG_W), F32),
                   jax.ShapeDtypeStruct((MLA_HEADS, t_tok, QK_PAD), BF16),
                   jax.ShapeDtypeStruct((MLA_HEADS, t_tok, QK_PAD), BF16),
                   jax.ShapeDtypeStruct((t_tok, MLA_HEADS * MLA_V), BF16)],
        compiler_params=pltpu.CompilerParams(dimension_semantics=("arbitrary",),
                                             vmem_limit_bytes=VMEM_LIMIT),
        name="input_stage",
    )(x2, pos, *consts)


def _hg_levels():
    sizes = []
    half = HG_CHUNK // 2
    while half >= HG_LEAF:
        sizes.append(half)
        half //= 2
    return tuple(sizes)


def _hg_chunks(jobs, lvl_masks):
    c = HG_CHUNK
    for j in jobs:
        half_gap = 0.5 * (1.0 - j["lb"])
        th = jnp.tanh(0.5 * j["logit"])
        f = (j["lb"] + half_gap) + half_gap * th
        j["k"] = half_gap - half_gap * th
        g = jnp.log2(f)
        g_hi = g.astype(BF16)
        r1 = g - g_hi.astype(F32)
        g_mid = r1.astype(BF16)
        g_lo = (r1 - g_mid.astype(F32)).astype(BF16)
        j["b"] = _dot(j["tri"], jnp.concatenate([g_hi, g_mid, g_lo], axis=0))

    for j in jobs:
        q, k, b = j["q"], j["k"], j["b"]
        b_tot = b[0:1] if j["rev"] else b[c - 1:c]
        j["dec"] = jnp.exp2(b_tot)
        j["q_in"] = (q * jnp.exp2(b)).astype(BF16)
        k_out = (k * jnp.exp2(b_tot - b)).astype(BF16)
        j["st_add"] = _dot(j["v"].T.astype(BF16), k_out)
        j["kpad"][HG_LEAF:HG_LEAF + c, :] = k
        j["bpad"][HG_LEAF:HG_LEAF + c, :] = b
        j["p"] = None

    for lvl, half in enumerate(_hg_levels()):
        n_grp = c // (2 * half)
        shape3 = (n_grp, 2 * half, HG_D)
        zero = jnp.zeros((n_grp, half, HG_D), F32)
        for j in jobs:
            b3, q3, k3 = j["b"].reshape(shape3), j["q"].reshape(shape3), j["k"].reshape(shape3)
            if j["rev"]:
                mid = b3[:, half:half + 1]
                q_l = jnp.concatenate([q3[:, :half] * jnp.exp2(b3[:, :half] - mid), zero], axis=1)
                k_l = jnp.concatenate([zero, k3[:, half:] * jnp.exp2(mid - b3[:, half:])], axis=1)
            else:
                mid = b3[:, half - 1:half]
                q_l = jnp.concatenate([zero, q3[:, half:] * jnp.exp2(b3[:, half:] - mid)], axis=1)
                k_l = jnp.concatenate([k3[:, :half] * jnp.exp2(mid - b3[:, :half]), zero], axis=1)
            p_l = _dot_nt(q_l.reshape(c, HG_D).astype(BF16), k_l.reshape(c, HG_D).astype(BF16))
            p_l = p_l * lvl_masks[lvl] if n_grp > 1 else p_l
            j["p"] = p_l if j["p"] is None else j["p"] + p_l

    for j in jobs:
        w = [(j["q"] * j["k"]).astype(BF16)]
        for d in range(1, HG_LEAF):
            lo = HG_LEAF + d if j["rev"] else HG_LEAF - d
            k_s = j["kpad"][lo:lo + c, :]
            b_s = j["bpad"][lo:lo + c, :]
            w.append((j["q"] * k_s * jnp.exp2(j["b"] - b_s)).astype(BF16))
        j["row_sums"] = _dot(jnp.concatenate(w, axis=1), j["spread"])

    outs = []
    for j in jobs:
        p_leaf = pltpu.roll(j["row_sums"], 0, 1, stride=1, stride_axis=0) * j["leaf_mask"]
        p = (j["p"] + p_leaf).astype(BF16)
        outs.append((_dot(p, j["v"].astype(BF16)), j["q_in"], j["st_add"], j["dec"]))
    return outs


def _hgrn_kernel(q_ref, i_ref, g_ref, ff_ref, fb_ref, lbp_ref, gain_ref, tri_ref, lvl_ref, leaf_ref,
                 spread_ref, out_ref, o_ref, qin_ref, add_ref, dec_ref, st_ref, kpad_ref, bpad_ref):
    s_len = q_ref.shape[0]
    n_chunk = s_len // HG_CHUNK
    logit_refs = (ff_ref, fb_ref)
    chunk_rows = lambda ci: pl.ds(pl.multiple_of(ci * HG_CHUNK, HG_CHUNK), HG_CHUNK)

    def lower_bound(d):
        p = lbp_ref[d]
        e = jnp.exp(p - jnp.max(p, axis=0, keepdims=True))
        return e[0:1] / jnp.sum(e, axis=0, keepdims=True)

    lbs = (lower_bound(0), lower_bound(1))
    kpad_ref[...] = jnp.zeros_like(kpad_ref)
    bpad_ref[...] = jnp.zeros_like(bpad_ref)

    def within_chunks(n, carry):
        lvl_masks = [lvl_ref[l] for l in range(lvl_ref.shape[0])]
        jobs = []
        for u in range(HG_UNROLL):
            ci = n * HG_UNROLL + u
            rows = chunk_rows(ci)
            for d, rev in enumerate((False, True)):
                slot = d * HG_UNROLL + u
                jobs.append(dict(
                    ci=ci, rows=rows, d=d, rev=rev, lb=lbs[d], tri=tri_ref[d],
                    q=q_ref[rows, :].astype(F32), logit=logit_refs[d][rows, :],
                    v=i_ref[rows, :].astype(F32), kpad=kpad_ref.at[slot], bpad=bpad_ref.at[slot],
                    leaf_mask=leaf_ref[d], spread=spread_ref[d]))
        for j, (o, q_in, st_add, dec) in zip(jobs, _hg_chunks(jobs, lvl_masks)):
            o_ref[j["d"], j["rows"], :] = o
            qin_ref[j["rows"], j["d"] * HG_D:(j["d"] + 1) * HG_D] = q_in
            add_ref[j["d"], j["ci"]] = st_add
            dec_ref[j["d"], j["ci"]] = dec
        return carry

    lax.fori_loop(0, n_chunk // HG_UNROLL, within_chunks, 0)

    def across_chunks(n, states):
        new_states = []
        for d, st in enumerate(states):
            ci = (n_chunk - 1 - n) if d else n
            st_ref[ci, :, d * HG_D:(d + 1) * HG_D] = st.astype(BF16)
            new_states.append(dec_ref[d, ci] * st + add_ref[d, ci])
        return tuple(new_states)

    zero_state = jnp.zeros((HG_D, HG_D), F32)
    lax.fori_loop(0, n_chunk, across_chunks, (zero_state, zero_state))

    def finish(n, carry):
        rows = chunk_rows(n)
        o = o_ref[0, rows, :] + o_ref[1, rows, :] + _dot_nt(qin_ref[rows, :], st_ref[n])
        y = _rms(o, gain_ref[...])
        gate = g_ref[rows, :].astype(F32)
        out_ref[rows, :] = (y * gate * _sigmoid(gate)).astype(BF16)
        return carry

    lax.fori_loop(0, n_chunk, finish, 0, unroll=4)


def _hgrn_stage(qig, fgate, lb_param, gain, batch, s_len):
    assert HG_CHUNK == LANES, "the leaf path rotates each row of a (C, C) tile by its row index"
    t_tok = qig.shape[0]
    c = HG_CHUNK
    r = jnp.arange(c)
    t_i, s_i = r[:, None], r[None, :]
    lower = s_i <= t_i
    tri = jnp.stack([jnp.tile(lower, (1, 3)), jnp.tile(lower.T, (1, 3))]).astype(BF16)
    lvl_masks = jnp.stack([t_i // (2 * h) == s_i // (2 * h) for h in _hg_levels()]).astype(F32)
    same_leaf = t_i // HG_LEAF == s_i // HG_LEAF
    leaf_mask = jnp.stack([same_leaf & lower, same_leaf & lower.T]).astype(F32)
    dist = jnp.repeat(jnp.arange(HG_LEAF), HG_D)[:, None]
    spread = jnp.stack([(-s_i) % HG_LEAF == dist, s_i % HG_LEAF == dist]).astype(BF16)
    blk = lambda col: pl.BlockSpec((s_len, HG_D), lambda b, h, col=col: (b, col * HG_HEADS + h))
    return pl.pallas_call(
        _hgrn_kernel,
        grid=(batch, HG_HEADS),
        in_specs=[blk(0), blk(1), blk(2), blk(0), blk(1),
                  pl.BlockSpec((2, 2, HG_D), lambda b, h: (0, 0, h)),
                  pl.BlockSpec((1, HG_D), lambda b, h: (0, h)),
                  _const_spec(tri.shape), _const_spec(lvl_masks.shape), _const_spec(leaf_mask.shape),
                  _const_spec(spread.shape)],
        out_specs=pl.BlockSpec((s_len, HG_D), lambda b, h: (b, h)),
        out_shape=jax.ShapeDtypeStruct((t_tok, HG_W), BF16),
        scratch_shapes=[pltpu.VMEM((2, s_len, HG_D), F32), pltpu.VMEM((s_len, 2 * HG_D), BF16),
                        pltpu.VMEM((2, s_len // c, HG_D, HG_D), F32),
                        pltpu.VMEM((2, s_len // c, 1, HG_D), F32),
                        pltpu.VMEM((s_len // c, HG_D, 2 * HG_D), BF16),
                        pltpu.VMEM((2 * HG_UNROLL, c + 2 * HG_LEAF, HG_D), F32),
                        pltpu.VMEM((2 * HG_UNROLL, c + 2 * HG_LEAF, HG_D), F32)],
        compiler_params=pltpu.CompilerParams(dimension_semantics=("arbitrary", "arbitrary"),
                                             vmem_limit_bytes=VMEM_LIMIT),
        name="hgrn_stage",
    )(qig, qig, qig, fgate, fgate, lb_param, gain, tri, lvl_masks, leaf_mask, spread)


def _attn_kernel(q_ref, k_ref, v_ref, o_ref):
    sub = min(ATTN_SUB, q_ref.shape[1])
    n_sub = q_ref.shape[1] // sub
    rows = lambda i: slice(i * sub, (i + 1) * sub)
    k, v = k_ref[0], v_ref[...]
    scores = lambda i: _dot_nt(q_ref[0, rows(i), :], k)
    pending = [scores(i) for i in range(min(ATTN_DEPTH, n_sub))]
    for i in range(n_sub):
        s = pending.pop(0)
        if i + ATTN_DEPTH < n_sub:
            pending.append(scores(i + ATTN_DEPTH))
        p = jnp.exp2(s - jnp.max(s, axis=-1, keepdims=True))
        o = _dot(p.astype(BF16), v)
        o_ref[rows(i), :] = (o * (1.0 / jnp.sum(p, axis=-1, keepdims=True))).astype(BF16)


def _attn_stage(q_all, k_all, v, batch, s_len, tq):
    t_tok = v.shape[0]
    nq = s_len // tq
    return pl.pallas_call(
        _attn_kernel,
        grid=(batch, MLA_HEADS, nq),
        in_specs=[pl.BlockSpec((1, tq, QK_PAD), lambda b, h, i: (h, b * nq + i, 0)),
                  pl.BlockSpec((1, s_len, QK_PAD), lambda b, h, i: (h, b, 0)),
                  pl.BlockSpec((s_len, MLA_V), lambda b, h, i: (b, h))],
        out_specs=pl.BlockSpec((tq, MLA_V), lambda b, h, i: (b * nq + i, h)),
        out_shape=jax.ShapeDtypeStruct((t_tok, MLA_HEADS * MLA_V), BF16),
        compiler_params=pltpu.CompilerParams(dimension_semantics=("arbitrary",) * 3,
                                             vmem_limit_bytes=VMEM_LIMIT),
        name="attn_stage",
    )(q_all, k_all, v)


def _output_kernel(x_ref, a_ref, ob_ref, g_mla_ref, w_oa_ref, w_ob_ref, g_ffn_ref, w_up_ref,
                   w_dn_ref, out_ref):
    ob = _rms(ob_ref[...].astype(F32), g_mla_ref[...]).astype(BF16)
    x1 = x_ref[...] + _dot(a_ref[...], w_oa_ref[...]) + _dot(ob, w_ob_ref[...])
    h = _rms(x1, g_ffn_ref[...]).astype(BF16)
    up = jnp.maximum(_dot(h, w_up_ref[...]), 0.0)
    out_ref[...] = x1 + _dot((up * up).astype(BF16), w_dn_ref[...])


def _output_stage(x2, a, ob, g_mla, w_oa, w_ob, g_ffn, w_up, w_dn, tm):
    t_tok, d_model = x2.shape
    consts = (g_mla, w_oa, w_ob, g_ffn, w_up, w_dn)
    return pl.pallas_call(
        _output_kernel,
        grid=(t_tok // tm,),
        in_specs=[pl.BlockSpec((tm, d_model), lambda i: (i, 0)),
                  pl.BlockSpec((tm, HG_W), lambda i: (i, 0)),
                  pl.BlockSpec((tm, MLA_HEADS * MLA_V), lambda i: (i, 0))]
                 + [_const_spec(c.shape) for c in consts],
        out_specs=pl.BlockSpec((tm, d_model), lambda i: (i, 0)),
        out_shape=jax.ShapeDtypeStruct((t_tok, d_model), F32),
        compiler_params=pltpu.CompilerParams(dimension_semantics=("arbitrary",),
                                             vmem_limit_bytes=VMEM_LIMIT),
        name="output_stage",
    )(x2, a, ob, *consts)


def kernel(x, positions, g_mix_norm, w_in, lb_param, g_hgrn_out, g_cq, w_q_up, g_ckv, w_kv_up,
           g_q_norm, g_k_norm, g_mla_out, w_out, g_ffn_norm, w_up, w_down):
    batch, s_len, d_model = x.shape
    assert g_mix_norm.shape[0] == 1 and lb_param.shape[1] == 2, "one layer"
    assert s_len % (HG_CHUNK * HG_UNROLL) == 0
    t_tok = batch * s_len
    row = lambda t: t.reshape(1, -1).astype(F32)

    w = w_in[0]
    o_q, o_ff, o_fb, o_i, o_g, o_cq = (HG_W * n for n in range(6))
    o_ckv = o_cq + Q_LORA
    o_kr = o_ckv + KV_LORA
    w_qig = jnp.concatenate([w[:, o_q:o_ff], w[:, o_i:o_g], w[:, o_g:o_cq]], axis=1).astype(BF16)
    w_f = w[:, o_ff:o_i].astype(BF16)
    half = MLA_ROPE // 2
    swap = lambda t: jnp.concatenate([t[..., half:], t[..., :half]], axis=-1)
    pad_lanes = lambda t: jnp.pad(t, [(0, 0)] * (t.ndim - 1) + [(0, LANES - MLA_ROPE)])
    w_kr = w[:, o_kr:]
    w_mla = jnp.concatenate([w[:, o_cq:o_kr], pad_lanes(w_kr), pad_lanes(swap(w_kr))],
                            axis=1).astype(BF16)

    wq = w_q_up[0].reshape(Q_LORA, MLA_HEADS, MLA_QK)
    wq_rope = wq[:, :, MLA_NOPE:]
    w_qup = jnp.concatenate([wq[:, :, :MLA_NOPE].reshape(Q_LORA, -1),
                             pad_lanes(wq_rope).reshape(Q_LORA, -1),
                             pad_lanes(swap(wq_rope)).reshape(Q_LORA, -1)], axis=1).astype(BF16)
    wkv = w_kv_up[0].reshape(KV_LORA, MLA_HEADS, MLA_NOPE + MLA_V)
    w_kvup = jnp.concatenate([wkv[:, :, :MLA_NOPE].reshape(KV_LORA, -1),
                              wkv[:, :, MLA_NOPE:].reshape(KV_LORA, -1)], axis=1).astype(BF16)
    rope_gain = lambda t: jnp.stack([pad_lanes(t), pad_lanes(swap(t))]).astype(F32)
    g_qn, g_qr = row(g_q_norm[0, :MLA_NOPE]), rope_gain(g_q_norm[0, MLA_NOPE:])
    g_kn, g_kr = row(g_k_norm[0, :MLA_NOPE]), rope_gain(g_k_norm[0, MLA_NOPE:])
    invf = (ROPE_THETA ** (-jnp.arange(0, MLA_ROPE, 2, dtype=F32) / MLA_ROPE)).reshape(-1, 1)

    x2 = x.reshape(t_tok, d_model)
    pos = positions.reshape(1, t_tok).astype(F32)
    tm = min(512, s_len)
    qig, fgate, q_all, k_all, v = _input_stage(
        x2, pos, invf, row(g_mix_norm[0]), w_qig, w_f, w_mla, row(g_cq[0]), w_qup, row(g_ckv[0]),
        w_kvup, g_qn, g_qr, g_kn, g_kr, tm)

    a = _hgrn_stage(qig, fgate, lb_param.astype(F32), g_hgrn_out[0].reshape(1, HG_W).astype(F32),
                    batch, s_len)
    ob = _attn_stage(q_all, k_all, v, batch, s_len, s_len)

    wo = w_out[0].astype(BF16)
    out = _output_stage(x2, a, ob, row(g_mla_out[0]), wo[:HG_W], wo[HG_W:], row(g_ffn_norm[0]),
                        w_up[0].astype(BF16), w_down[0].astype(BF16), tm)
    return out.reshape(batch, s_len, d_model)
```

```python
import functools

import jax
import jax.numpy as jnp
from jax import lax
from jax.experimental import pallas as pl
from jax.experimental.pallas import tpu as pltpu

NORM_EPS = 1e-6
HG_HEADS = 4
HG_D = 128
HG_W = HG_HEADS * HG_D
MLA_HEADS = 4
MLA_NOPE = 128
MLA_ROPE = 64
MLA_V = 128
MLA_QK = MLA_NOPE + MLA_ROPE
Q_LORA = 384
KV_LORA = 256
ROPE_THETA = 10000.0
LANES = 128
QK_PAD = 2 * LANES

HG_CHUNK = 128
HG_LEAF = 8
HG_BLOCK = 32
HG_MAX_BLOCK_DECAY = 100.0
HG_UNROLL = 4
ATTN_SUB = 512
ATTN_DEPTH = 1
INPUT_SUB = 256
LOG2_E = 1.4426950408889634
VMEM_LIMIT = 56 * 1024 * 1024

F32 = jnp.float32
BF16 = jnp.bfloat16


def _dot(a, b):
    return jnp.dot(a, b, preferred_element_type=F32)


def _dot_nt(a, b):
    return lax.dot_general(a, b, (((1,), (1,)), ((), ())), preferred_element_type=F32)


def _rms(t, gain):
    return t * lax.rsqrt(jnp.mean(t * t, axis=-1, keepdims=True) + NORM_EPS) * gain


def _sigmoid(t):
    return 1.0 / (1.0 + jnp.exp2(t * -LOG2_E))


def _input_kernel(x_ref, pos_ref, invf_ref, g_mix_ref, w_qig_ref, w_f_ref, w_mla_ref, g_cq_ref,
                  w_qup_ref, g_ckv_ref, w_kvup_ref, g_qn_ref, g_qr_ref, g_kn_ref, g_kr_ref,
                  qig_ref, f_ref, qa_ref, ka_ref, v_ref):
    tm = x_ref.shape[0]
    sub = min(INPUT_SUB, tm)
    tiles = [dict(rows=slice(i * sub, (i + 1) * sub)) for i in range(tm // sub)]
    scale = MLA_QK ** -0.5 * LOG2_E

    def project(t):
        h = _rms(x_ref[t["rows"], :], g_mix_ref[...]).astype(BF16)
        qig_ref[t["rows"], :] = _dot(h, w_qig_ref[...]).astype(BF16)
        f_ref[t["rows"], :] = _dot(h, w_f_ref[...])
        t["m"] = _dot(h, w_mla_ref[...])

    def up_project(t):
        c_q = t["m"][:, :Q_LORA]
        c_kv = t["m"][:, Q_LORA:Q_LORA + KV_LORA]
        t["q"] = _dot(_rms(c_q, g_cq_ref[...]).astype(BF16), w_qup_ref[...])
        t["kv"] = _dot(_rms(c_kv, g_ckv_ref[...]).astype(BF16), w_kvup_ref[...])

    def rope_tables():
        ang = invf_ref[...] * pos_ref[...]
        c, s = jnp.cos(ang), jnp.sin(ang)
        z = jnp.zeros_like(c)
        return jnp.concatenate([c, c, z, z], axis=0).T, jnp.concatenate([-s, s, z, z], axis=0).T

    def heads(t, cos_t, sin_t):
        rows, q, kv = t["rows"], t["q"], t["kv"]
        v_ref[rows, :] = kv[:, MLA_HEADS * MLA_NOPE:].astype(BF16)
        cos_r, sin_r = cos_t[rows, :], sin_t[rows, :]
        q_cos, q_sin = g_qr_ref[0:1] * cos_r, g_qr_ref[1:2] * sin_r
        rope_cols = Q_LORA + KV_LORA
        k_r = t["m"][:, rope_cols:rope_cols + LANES]
        k_pe = (k_r * (g_kr_ref[0:1] * cos_r)
                + t["m"][:, rope_cols + LANES:] * (g_kr_ref[1:2] * sin_r))
        sq_kr = k_r * k_r
        for hd in range(MLA_HEADS):
            lo, hi = hd * LANES, (hd + 1) * LANES
            q_n = q[:, lo:hi]
            q_r = q[:, MLA_HEADS * MLA_NOPE + lo:MLA_HEADS * MLA_NOPE + hi]
            q_r_swapped = q[:, 2 * MLA_HEADS * MLA_NOPE + lo:2 * MLA_HEADS * MLA_NOPE + hi]
            ssq = jnp.sum(q_n * q_n + q_r * q_r, axis=-1, keepdims=True)
            rstd = lax.rsqrt(ssq * (1.0 / MLA_QK) + NORM_EPS) * scale
            q_pe = q_r * q_cos + q_r_swapped * q_sin
            qa_ref[hd, rows, :] = jnp.concatenate([q_n * g_qn_ref[...] * rstd, q_pe * rstd],
                                                  axis=1).astype(BF16)
            k_n = kv[:, lo:hi]
            ssq = jnp.sum(k_n * k_n + sq_kr, axis=-1, keepdims=True)
            rstd = lax.rsqrt(ssq * (1.0 / MLA_QK) + NORM_EPS)
            ka_ref[hd, rows, :] = jnp.concatenate([k_n * g_kn_ref[...] * rstd, k_pe * rstd],
                                                  axis=1).astype(BF16)

    tables = None
    for i, t in enumerate(tiles):
        project(t)
        if i == 0:
            tables = rope_tables()
        else:
            heads(tiles[i - 1], *tables)
        up_project(t)
    heads(tiles[-1], *tables)


def _const_spec(shape):
    nd = len(shape)
    return pl.BlockSpec(shape, lambda *_: (0,) * nd, pipeline_mode=pl.Buffered(1))


def _input_stage(x2, pos, invf, g_mix, w_qig, w_f, w_mla, g_cq, w_qup, g_ckv, w_kvup,
                 g_qn, g_qr, g_kn, g_kr, tm):
    t_tok, d_model = x2.shape
    consts = (invf, g_mix, w_qig, w_f, w_mla, g_cq, w_qup, g_ckv, w_kvup, g_qn, g_qr, g_kn, g_kr)
    return pl.pallas_call(
        _input_kernel,
        grid=(t_tok // tm,),
        in_specs=[pl.BlockSpec((tm, d_model), lambda i: (i, 0)),
                  pl.BlockSpec((1, tm), lambda i: (0, i))] + [_const_spec(c.shape) for c in consts],
        out_specs=[pl.BlockSpec((tm, 3 * HG_W), lambda i: (i, 0)),
                   pl.BlockSpec((tm, 2 * HG_W), lambda i: (i, 0)),
                   pl.BlockSpec((MLA_HEADS, tm, QK_PAD), lambda i: (0, i, 0)),
                   pl.BlockSpec((MLA_HEADS, tm, QK_PAD), lambda i: (0, i, 0)),
                   pl.BlockSpec((tm, MLA_HEADS * MLA_V), lambda i: (i, 0))],
        out_shape=[jax.ShapeDtypeStruct((t_tok, 3 * HG_W), BF16),
                   jax.ShapeDtypeStruct((t_tok, 2 * HG_W), F32),
                   jax.ShapeDtypeStruct((MLA_HEADS, t_tok, QK_PAD), BF16),
                   jax.ShapeDtypeStruct((MLA_HEADS, t_tok, QK_PAD), BF16),
                   jax.ShapeDtypeStruct((t_tok, MLA_HEADS * MLA_V), BF16)],
        compiler_params=pltpu.CompilerParams(dimension_semantics=("arbitrary",),
                                             vmem_limit_bytes=VMEM_LIMIT),
        name="input_stage",
    )(x2, pos, *consts)


def _hg_levels(leaf):
    sizes = []
    half = HG_CHUNK // 2
    while half >= leaf:
        sizes.append(half)
        half //= 2
    return tuple(sizes)


def _hg_gates(logit, lb, tri):
    f = lb + (1.0 - lb) * _sigmoid(logit)
    k = 1.0 - f
    g = jnp.log2(f)
    g_hi = g.astype(BF16)
    r1 = g - g_hi.astype(F32)
    g_mid = r1.astype(BF16)
    g_lo = (r1 - g_mid.astype(F32)).astype(BF16)
    return k, _dot(tri, jnp.concatenate([g_hi, g_mid, g_lo], axis=0))


def _hg_block_decay(b, rev):
    n_blk = HG_CHUNK // HG_BLOCK
    edge = [b[i * HG_BLOCK:i * HG_BLOCK + 1] if rev else b[(i + 1) * HG_BLOCK - 1:(i + 1) * HG_BLOCK]
            for i in range(n_blk)]
    worst = edge[-1] if rev else edge[0]
    for i in range(1, n_blk):
        worst = jnp.minimum(worst, edge[i - 1] - edge[i] if rev else edge[i] - edge[i - 1])
    return worst


def _hg_chunks(jobs, lvl_masks, blocked):
    c = HG_CHUNK
    for j in jobs:
        q, k, b = j["q"], j["k"], j["b"]
        b_tot = b[0:1] if j["rev"] else b[c - 1:c]
        j["dec"] = jnp.exp2(b_tot)
        j["q_in"] = (q * jnp.exp2(b)).astype(BF16)
        k_out = (k * jnp.exp2(b_tot - b)).astype(BF16)
        j["st_add"] = _dot(j["v"].T.astype(BF16), k_out)
        j["p"] = None

    for lvl, half in enumerate(_hg_levels(HG_BLOCK if blocked else HG_LEAF)):
        n_grp = c // (2 * half)
        shape3 = (n_grp, 2 * half, HG_D)
        zero = jnp.zeros((n_grp, half, HG_D), F32)
        for j in jobs:
            b3, q3, k3 = j["b"].reshape(shape3), j["q"].reshape(shape3), j["k"].reshape(shape3)
            if j["rev"]:
                mid = b3[:, half:half + 1]
                q_l = jnp.concatenate([q3[:, :half] * jnp.exp2(b3[:, :half] - mid), zero], axis=1)
                k_l = jnp.concatenate([zero, k3[:, half:] * jnp.exp2(mid - b3[:, half:])], axis=1)
            else:
                mid = b3[:, half - 1:half]
                q_l = jnp.concatenate([zero, q3[:, half:] * jnp.exp2(b3[:, half:] - mid)], axis=1)
                k_l = jnp.concatenate([k3[:, :half] * jnp.exp2(mid - b3[:, :half]), zero], axis=1)
            p_l = _dot_nt(q_l.reshape(c, HG_D).astype(BF16), k_l.reshape(c, HG_D).astype(BF16))
            p_l = p_l * lvl_masks[lvl] if n_grp > 1 else p_l
            j["p"] = p_l if j["p"] is None else j["p"] + p_l

    if blocked:
        n_blk = c // HG_BLOCK
        shape3 = (n_blk, HG_BLOCK, HG_D)
        zero_row = jnp.zeros((1, 1, HG_D), F32)
        for j in jobs:
            b3 = j["b"].reshape(shape3)
            if j["rev"]:
                ref = jnp.concatenate([b3[1:, 0:1], zero_row], axis=0)
            else:
                ref = jnp.concatenate([zero_row, b3[:-1, HG_BLOCK - 1:HG_BLOCK]], axis=0)
            b_loc = b3 - ref
            q_b = (j["q"].reshape(shape3) * jnp.exp2(b_loc)).reshape(c, HG_D).astype(BF16)
            k_b = (j["k"].reshape(shape3) * jnp.exp2(-b_loc)).reshape(c, HG_D).astype(BF16)
            j["p_leaf"] = _dot_nt(q_b, k_b) * j["block_mask"]
    else:
        for j in jobs:
            j["kpad"][HG_LEAF:HG_LEAF + c, :] = j["k"]
            j["bpad"][HG_LEAF:HG_LEAF + c, :] = j["b"]
        for j in jobs:
            w = [(j["q"] * j["k"]).astype(BF16)]
            for d in range(1, HG_LEAF):
                lo = HG_LEAF + d if j["rev"] else HG_LEAF - d
                k_s = j["kpad"][lo:lo + c, :]
                b_s = j["bpad"][lo:lo + c, :]
                w.append((j["q"] * k_s * jnp.exp2(j["b"] - b_s)).astype(BF16))
            j["row_sums"] = _dot(jnp.concatenate(w, axis=1), j["spread"])
        for j in jobs:
            j["p_leaf"] = (pltpu.roll(j["row_sums"], 0, 1, stride=1, stride_axis=0)
                           * j["leaf_mask"])

    outs = []
    for j in jobs:
        p = (j["p"] + j["p_leaf"]).astype(BF16)
        outs.append((_dot(p, j["v"].astype(BF16)), j["q_in"], j["st_add"], j["dec"]))
    return outs


def _hgrn_kernel(q_ref, i_ref, g_ref, ff_ref, fb_ref, lbp_ref, gain_ref, tri_ref, lvl_ref, leaf_ref,
                 block_ref, spread_ref, out_ref, k_ref, b_ref, o_ref, qin_ref, add_ref, dec_ref,
                 st_ref, kpad_ref, bpad_ref):
    s_len = q_ref.shape[0]
    n_chunk = s_len // HG_CHUNK
    logit_refs = (ff_ref, fb_ref)
    chunk_rows = lambda ci: pl.ds(pl.multiple_of(ci * HG_CHUNK, HG_CHUNK), HG_CHUNK)

    def lower_bound(d):
        p = lbp_ref[d]
        e = jnp.exp(p - jnp.max(p, axis=0, keepdims=True))
        return e[0:1] / jnp.sum(e, axis=0, keepdims=True)

    lbs = (lower_bound(0), lower_bound(1))

    def gates(n, worst):
        for u in range(HG_UNROLL):
            rows = chunk_rows(n * HG_UNROLL + u)
            for d in range(2):
                k, b = _hg_gates(logit_refs[d][rows, :], lbs[d], tri_ref[d])
                k_ref[d, rows, :] = k
                b_ref[d, rows, :] = b
                worst = jnp.minimum(worst, _hg_block_decay(b, bool(d)))
        return worst

    worst = lax.fori_loop(0, n_chunk // HG_UNROLL, gates, jnp.zeros((1, HG_D), F32))
    blocks_representable = jnp.min(worst) >= -HG_MAX_BLOCK_DECAY

    def within_chunks(blocked):
        def body(n, carry):
            lvl_masks = [lvl_ref[l] for l in range(lvl_ref.shape[0])]
            jobs = []
            for u in range(HG_UNROLL):
                ci = n * HG_UNROLL + u
                rows = chunk_rows(ci)
                for d, rev in enumerate((False, True)):
                    slot = d * HG_UNROLL + u
                    jobs.append(dict(
                        ci=ci, rows=rows, d=d, rev=rev, q=q_ref[rows, :].astype(F32),
                        k=k_ref[d, rows, :], b=b_ref[d, rows, :], v=i_ref[rows, :].astype(F32),
                        kpad=kpad_ref.at[slot], bpad=bpad_ref.at[slot], leaf_mask=leaf_ref[d],
                        block_mask=block_ref[d], spread=spread_ref[d]))
            for j, (o, q_in, st_add, dec) in zip(jobs, _hg_chunks(jobs, lvl_masks, blocked)):
                o_ref[j["d"], j["rows"], :] = o
                qin_ref[j["rows"], j["d"] * HG_D:(j["d"] + 1) * HG_D] = q_in
                add_ref[j["d"], j["ci"]] = st_add
                dec_ref[j["d"], j["ci"]] = dec
            return carry
        return body

    @pl.when(blocks_representable)
    def _():
        lax.fori_loop(0, n_chunk // HG_UNROLL, within_chunks(True), 0)

    @pl.when(jnp.logical_not(blocks_representable))
    def _():
        kpad_ref[...] = jnp.zeros_like(kpad_ref)
        bpad_ref[...] = jnp.zeros_like(bpad_ref)
        lax.fori_loop(0, n_chunk // HG_UNROLL, within_chunks(False), 0)

    def across_chunks(n, states):
        new_states = []
        for d, st in enumerate(states):
            ci = (n_chunk - 1 - n) if d else n
            st_ref[ci, :, d * HG_D:(d + 1) * HG_D] = st.astype(BF16)
            new_states.append(dec_ref[d, ci] * st + add_ref[d, ci])
        return tuple(new_states)

    zero_state = jnp.zeros((HG_D, HG_D), F32)
    lax.fori_loop(0, n_chunk, across_chunks, (zero_state, zero_state))

    def finish(n, carry):
        rows = chunk_rows(n)
        o = o_ref[0, rows, :] + o_ref[1, rows, :] + _dot_nt(qin_ref[rows, :], st_ref[n])
        y = _rms(o, gain_ref[...])
        gate = g_ref[rows, :].astype(F32)
        out_ref[rows, :] = (y * gate * _sigmoid(gate)).astype(BF16)
        return carry

    lax.fori_loop(0, n_chunk, finish, 0, unroll=4)


def _hgrn_stage(qig, fgate, lb_param, gain, batch, s_len):
    assert HG_CHUNK == LANES, "the leaf path rotates each row of a (C, C) tile by its row index"
    t_tok = qig.shape[0]
    c = HG_CHUNK
    r = jnp.arange(c)
    t_i, s_i = r[:, None], r[None, :]
    lower = s_i <= t_i
    causal = jnp.stack([lower, lower.T])
    tri = jnp.tile(causal, (1, 1, 3)).astype(BF16)
    lvl_masks = jnp.stack([t_i // (2 * h) == s_i // (2 * h) for h in _hg_levels(HG_LEAF)]).astype(F32)
    leaf_mask = (causal & (t_i // HG_LEAF == s_i // HG_LEAF)).astype(F32)
    block_mask = (causal & (t_i // HG_BLOCK == s_i // HG_BLOCK)).astype(F32)
    dist = jnp.repeat(jnp.arange(HG_LEAF), HG_D)[:, None]
    spread = jnp.stack([(-s_i) % HG_LEAF == dist, s_i % HG_LEAF == dist]).astype(BF16)
    blk = lambda col: pl.BlockSpec((s_len, HG_D), lambda b, h, col=col: (b, col * HG_HEADS + h))
    consts = (tri, lvl_masks, leaf_mask, block_mask, spread)
    seq = lambda dtype: pltpu.VMEM((2, s_len, HG_D), dtype)
    return pl.pallas_call(
        _hgrn_kernel,
        grid=(batch, HG_HEADS),
        in_specs=[blk(0), blk(1), blk(2), blk(0), blk(1),
                  pl.BlockSpec((2, 2, HG_D), lambda b, h: (0, 0, h)),
                  pl.BlockSpec((1, HG_D), lambda b, h: (0, h))] + [_const_spec(t.shape) for t in consts],
        out_specs=pl.BlockSpec((s_len, HG_D), lambda b, h: (b, h)),
        out_shape=jax.ShapeDtypeStruct((t_tok, HG_W), BF16),
        scratch_shapes=[seq(F32), seq(F32), seq(F32), pltpu.VMEM((s_len, 2 * HG_D), BF16),
                        pltpu.VMEM((2, s_len // c, HG_D, HG_D), F32),
                        pltpu.VMEM((2, s_len // c, 1, HG_D), F32),
                        pltpu.VMEM((s_len // c, HG_D, 2 * HG_D), BF16),
                        pltpu.VMEM((2 * HG_UNROLL, c + 2 * HG_LEAF, HG_D), F32),
                        pltpu.VMEM((2 * HG_UNROLL, c + 2 * HG_LEAF, HG_D), F32)],
        compiler_params=pltpu.CompilerParams(dimension_semantics=("arbitrary", "arbitrary"),
                                             vmem_limit_bytes=VMEM_LIMIT),
        name="hgrn_stage",
    )(qig, qig, qig, fgate, fgate, lb_param, gain, *consts)


def _attn_kernel(q_ref, k_ref, v_ref, o_ref):
    sub = min(ATTN_SUB, q_ref.shape[1])
    n_sub = q_ref.shape[1] // sub
    rows = lambda i: slice(i * sub, (i + 1) * sub)
    k, v = k_ref[0], v_ref[...]
    scores = lambda i: _dot_nt(q_ref[0, rows(i), :], k)
    pending = [scores(i) for i in range(min(ATTN_DEPTH, n_sub))]
    for i in range(n_sub):
        s = pending.pop(0)
        if i + ATTN_DEPTH < n_sub:
            pending.append(scores(i + ATTN_DEPTH))
        p = jnp.exp2(s - jnp.max(s, axis=-1, keepdims=True))
        o = _dot(p.astype(BF16), v)
        o_ref[rows(i), :] = (o * (1.0 / jnp.sum(p, axis=-1, keepdims=True))).astype(BF16)


def _attn_stage(q_all, k_all, v, batch, s_len, tq):
    t_tok = v.shape[0]
    nq = s_len // tq
    return pl.pallas_call(
        _attn_kernel,
        grid=(batch, MLA_HEADS, nq),
        in_specs=[pl.BlockSpec((1, tq, QK_PAD), lambda b, h, i: (h, b * nq + i, 0)),
                  pl.BlockSpec((1, s_len, QK_PAD), lambda b, h, i: (h, b, 0)),
                  pl.BlockSpec((s_len, MLA_V), lambda b, h, i: (b, h))],
        out_specs=pl.BlockSpec((tq, MLA_V), lambda b, h, i: (b * nq + i, h)),
        out_shape=jax.ShapeDtypeStruct((t_tok, MLA_HEADS * MLA_V), BF16),
        compiler_params=pltpu.CompilerParams(dimension_semantics=("arbitrary",) * 3,
                                             vmem_limit_bytes=VMEM_LIMIT),
        name="attn_stage",
    )(q_all, k_all, v)


def _output_kernel(x_ref, a_ref, ob_ref, g_mla_ref, w_oa_ref, w_ob_ref, g_ffn_ref, w_up_ref,
                   w_dn_ref, out_ref):
    ob = _rms(ob_ref[...].astype(F32), g_mla_ref[...]).astype(BF16)
    x1 = x_ref[...] + _dot(a_ref[...], w_oa_ref[...]) + _dot(ob, w_ob_ref[...])
    h = _rms(x1, g_ffn_ref[...]).astype(BF16)
    up = jnp.maximum(_dot(h, w_up_ref[...]), 0.0)
    out_ref[...] = x1 + _dot((up * up).astype(BF16), w_dn_ref[...])


def _output_stage(x2, a, ob, g_mla, w_oa, w_ob, g_ffn, w_up, w_dn, tm):
    t_tok, d_model = x2.shape
    consts = (g_mla, w_oa, w_ob, g_ffn, w_up, w_dn)
    return pl.pallas_call(
        _output_kernel,
        grid=(t_tok // tm,),
        in_specs=[pl.BlockSpec((tm, d_model), lambda i: (i, 0)),
                  pl.BlockSpec((tm, HG_W), lambda i: (i, 0)),
                  pl.BlockSpec((tm, MLA_HEADS * MLA_V), lambda i: (i, 0))]
                 + [_const_spec(c.shape) for c in consts],
        out_specs=pl.BlockSpec((tm, d_model), lambda i: (i, 0)),
        out_shape=jax.ShapeDtypeStruct((t_tok, d_model), F32),
        compiler_params=pltpu.CompilerParams(dimension_semantics=("arbitrary",),
                                             vmem_limit_bytes=VMEM_LIMIT),
        name="output_stage",
    )(x2, a, ob, *consts)


def kernel(x, positions, g_mix_norm, w_in, lb_param, g_hgrn_out, g_cq, w_q_up, g_ckv, w_kv_up,
           g_q_norm, g_k_norm, g_mla_out, w_out, g_ffn_norm, w_up, w_down):
    batch, s_len, d_model = x.shape
    assert g_mix_norm.shape[0] == 1 and lb_param.shape[1] == 2, "one layer"
    assert s_len % (HG_CHUNK * HG_UNROLL) == 0
    t_tok = batch * s_len
    row = lambda t: t.reshape(1, -1).astype(F32)

    w = w_in[0]
    o_q, o_ff, o_fb, o_i, o_g, o_cq = (HG_W * n for n in range(6))
    o_ckv = o_cq + Q_LORA
    o_kr = o_ckv + KV_LORA
    w_qig = jnp.concatenate([w[:, o_q:o_ff], w[:, o_i:o_g], w[:, o_g:o_cq]], axis=1).astype(BF16)
    w_f = w[:, o_ff:o_i].astype(BF16)
    half = MLA_ROPE // 2
    swap = lambda t: jnp.concatenate([t[..., half:], t[..., :half]], axis=-1)
    pad_lanes = lambda t: jnp.pad(t, [(0, 0)] * (t.ndim - 1) + [(0, LANES - MLA_ROPE)])
    w_kr = w[:, o_kr:]
    w_mla = jnp.concatenate([w[:, o_cq:o_kr], pad_lanes(w_kr), pad_lanes(swap(w_kr))],
                            axis=1).astype(BF16)

    wq = w_q_up[0].reshape(Q_LORA, MLA_HEADS, MLA_QK)
    wq_rope = wq[:, :, MLA_NOPE:]
    w_qup = jnp.concatenate([wq[:, :, :MLA_NOPE].reshape(Q_LORA, -1),
                             pad_lanes(wq_rope).reshape(Q_LORA, -1),
                             pad_lanes(swap(wq_rope)).reshape(Q_LORA, -1)], axis=1).astype(BF16)
    wkv = w_kv_up[0].reshape(KV_LORA, MLA_HEADS, MLA_NOPE + MLA_V)
    w_kvup = jnp.concatenate([wkv[:, :, :MLA_NOPE].reshape(KV_LORA, -1),
                              wkv[:, :, MLA_NOPE:].reshape(KV_LORA, -1)], axis=1).astype(BF16)
    rope_gain = lambda t: jnp.stack([pad_lanes(t), pad_lanes(swap(t))]).astype(F32)
    g_qn, g_qr = row(g_q_norm[0, :MLA_NOPE]), rope_gain(g_q_norm[0, MLA_NOPE:])
    g_kn, g_kr = row(g_k_norm[0, :MLA_NOPE]), rope_gain(g_k_norm[0, MLA_NOPE:])
    invf = (ROPE_THETA ** (-jnp.arange(0, MLA_ROPE, 2, dtype=F32) / MLA_ROPE)).reshape(-1, 1)

    x2 = x.reshape(t_tok, d_model)
    pos = positions.reshape(1, t_tok).astype(F32)
    tm = min(512, s_len)
    qig, fgate, q_all, k_all, v = _input_stage(
        x2, pos, invf, row(g_mix_norm[0]), w_qig, w_f, w_mla, row(g_cq[0]), w_qup, row(g_ckv[0]),
        w_kvup, g_qn, g_qr, g_kn, g_kr, tm)

    a = _hgrn_stage(qig, fgate, lb_param.astype(F32), g_hgrn_out[0].reshape(1, HG_W).astype(F32),
                    batch, s_len)
    ob = _attn_stage(q_all, k_all, v, batch, s_len, s_len)

    wo = w_out[0].astype(BF16)
    out = _output_stage(x2, a, ob, row(g_mla_out[0]), wo[:HG_W], wo[HG_W:], row(g_ffn_norm[0]),
                        w_up[0].astype(BF16), w_down[0].astype(BF16), tm)
    return out.reshape(batch, s_len, d_model)
```

```python
import functools

import jax
import jax.numpy as jnp
from jax import lax
from jax.experimental import pallas as pl
from jax.experimental.pallas import tpu as pltpu

NORM_EPS = 1e-6
HG_HEADS = 4
HG_D = 128
HG_W = HG_HEADS * HG_D
MLA_HEADS = 4
MLA_NOPE = 128
MLA_ROPE = 64
MLA_V = 128
MLA_QK = MLA_NOPE + MLA_ROPE
Q_LORA = 384
KV_LORA = 256
ROPE_THETA = 10000.0
LANES = 128
QK_PAD = 2 * LANES

HG_CHUNK = 128
HG_LEAF = 8
HG_BLOCK = 32
HG_MAX_BLOCK_DECAY = 100.0
HG_UNROLL = 8
ATTN_SUB = 512
ATTN_HEADS = 2
ATTN_DEPTH = 1
INPUT_SUB = 256
LOG2_E = 1.4426950408889634
VMEM_LIMIT = 56 * 1024 * 1024

F32 = jnp.float32
BF16 = jnp.bfloat16


def _dot(a, b):
    return jnp.dot(a, b, preferred_element_type=F32)


def _dot_nt(a, b):
    return lax.dot_general(a, b, (((1,), (1,)), ((), ())), preferred_element_type=F32)


def _rms(t, gain):
    return t * lax.rsqrt(jnp.mean(t * t, axis=-1, keepdims=True) + NORM_EPS) * gain


def _sigmoid(t):
    return 1.0 / (1.0 + jnp.exp2(t * -LOG2_E))


def _input_kernel(x_ref, pos_ref, invf_ref, g_mix_ref, w_qig_ref, w_f_ref, w_mla_ref, g_cq_ref,
                  w_qup_ref, g_ckv_ref, w_kvup_ref, g_qn_ref, g_qr_ref, g_kn_ref, g_kr_ref,
                  qig_ref, f_ref, qa_ref, ka_ref, v_ref):
    tm = x_ref.shape[0]
    sub = min(INPUT_SUB, tm)
    tiles = [dict(rows=slice(i * sub, (i + 1) * sub)) for i in range(tm // sub)]
    scale = MLA_QK ** -0.5 * LOG2_E

    def project(t):
        h = _rms(x_ref[t["rows"], :], g_mix_ref[...]).astype(BF16)
        qig_ref[t["rows"], :] = _dot(h, w_qig_ref[...]).astype(BF16)
        f_ref[t["rows"], :] = _dot(h, w_f_ref[...])
        t["m"] = _dot(h, w_mla_ref[...])

    def up_project(t):
        c_q = t["m"][:, :Q_LORA]
        c_kv = t["m"][:, Q_LORA:Q_LORA + KV_LORA]
        t["q"] = _dot(_rms(c_q, g_cq_ref[...]).astype(BF16), w_qup_ref[...])
        t["kv"] = _dot(_rms(c_kv, g_ckv_ref[...]).astype(BF16), w_kvup_ref[...])

    def rope_tables():
        ang = invf_ref[...] * pos_ref[...]
        c, s = jnp.cos(ang), jnp.sin(ang)
        z = jnp.zeros_like(c)
        return jnp.concatenate([c, c, z, z], axis=0).T, jnp.concatenate([-s, s, z, z], axis=0).T

    def heads(t, cos_t, sin_t):
        rows, q, kv = t["rows"], t["q"], t["kv"]
        v_ref[rows, :] = kv[:, MLA_HEADS * MLA_NOPE:].astype(BF16)
        cos_r, sin_r = cos_t[rows, :], sin_t[rows, :]
        q_cos, q_sin = g_qr_ref[0:1] * cos_r, g_qr_ref[1:2] * sin_r
        rope_cols = Q_LORA + KV_LORA
        k_r = t["m"][:, rope_cols:rope_cols + LANES]
        k_pe = (k_r * (g_kr_ref[0:1] * cos_r)
                + t["m"][:, rope_cols + LANES:] * (g_kr_ref[1:2] * sin_r))
        sq_kr = k_r * k_r
        for hd in range(MLA_HEADS):
            lo, hi = hd * LANES, (hd + 1) * LANES
            q_n = q[:, lo:hi]
            q_r = q[:, MLA_HEADS * MLA_NOPE + lo:MLA_HEADS * MLA_NOPE + hi]
            q_r_swapped = q[:, 2 * MLA_HEADS * MLA_NOPE + lo:2 * MLA_HEADS * MLA_NOPE + hi]
            ssq = jnp.sum(q_n * q_n + q_r * q_r, axis=-1, keepdims=True)
            rstd = lax.rsqrt(ssq * (1.0 / MLA_QK) + NORM_EPS) * scale
            q_pe = q_r * q_cos + q_r_swapped * q_sin
            qa_ref[hd, rows, :] = jnp.concatenate([q_n * g_qn_ref[...] * rstd, q_pe * rstd],
                                                  axis=1).astype(BF16)
            k_n = kv[:, lo:hi]
            ssq = jnp.sum(k_n * k_n + sq_kr, axis=-1, keepdims=True)
            rstd = lax.rsqrt(ssq * (1.0 / MLA_QK) + NORM_EPS)
            ka_ref[hd, rows, :] = jnp.concatenate([k_n * g_kn_ref[...] * rstd, k_pe * rstd],
                                                  axis=1).astype(BF16)

    tables = None
    for i, t in enumerate(tiles):
        project(t)
        if i == 0:
            tables = rope_tables()
        else:
            heads(tiles[i - 1], *tables)
        up_project(t)
    heads(tiles[-1], *tables)


def _const_spec(shape):
    nd = len(shape)
    return pl.BlockSpec(shape, lambda *_: (0,) * nd, pipeline_mode=pl.Buffered(1))


def _input_stage(x2, pos, invf, g_mix, w_qig, w_f, w_mla, g_cq, w_qup, g_ckv, w_kvup,
                 g_qn, g_qr, g_kn, g_kr, tm):
    t_tok, d_model = x2.shape
    consts = (invf, g_mix, w_qig, w_f, w_mla, g_cq, w_qup, g_ckv, w_kvup, g_qn, g_qr, g_kn, g_kr)
    return pl.pallas_call(
        _input_kernel,
        grid=(t_tok // tm,),
        in_specs=[pl.BlockSpec((tm, d_model), lambda i: (i, 0)),
                  pl.BlockSpec((1, tm), lambda i: (0, i))] + [_const_spec(c.shape) for c in consts],
        out_specs=[pl.BlockSpec((tm, 3 * HG_W), lambda i: (i, 0)),
                   pl.BlockSpec((tm, 2 * HG_W), lambda i: (i, 0)),
                   pl.BlockSpec((MLA_HEADS, tm, QK_PAD), lambda i: (0, i, 0)),
                   pl.BlockSpec((MLA_HEADS, tm, QK_PAD), lambda i: (0, i, 0)),
                   pl.BlockSpec((tm, MLA_HEADS * MLA_V), lambda i: (i, 0))],
        out_shape=[jax.ShapeDtypeStruct((t_tok, 3 * HG_W), BF16),
                   jax.ShapeDtypeStruct((t_tok, 2 * HG_W), F32),
                   jax.ShapeDtypeStruct((MLA_HEADS, t_tok, QK_PAD), BF16),
                   jax.ShapeDtypeStruct((MLA_HEADS, t_tok, QK_PAD), BF16),
                   jax.ShapeDtypeStruct((t_tok, MLA_HEADS * MLA_V), BF16)],
        compiler_params=pltpu.CompilerParams(dimension_semantics=("arbitrary",),
                                             vmem_limit_bytes=VMEM_LIMIT),
        name="input_stage",
    )(x2, pos, *consts)


def _hg_levels(leaf):
    sizes = []
    half = HG_CHUNK // 2
    while half >= leaf:
        sizes.append(half)
        half //= 2
    return tuple(sizes)


def _hg_gates(logit, lb, tri):
    f = lb + (1.0 - lb) * _sigmoid(logit)
    k = 1.0 - f
    g = jnp.log2(f)
    g_hi = g.astype(BF16)
    r1 = g - g_hi.astype(F32)
    g_mid = r1.astype(BF16)
    g_lo = (r1 - g_mid.astype(F32)).astype(BF16)
    return k, _dot(tri, jnp.concatenate([g_hi, g_mid, g_lo], axis=0))


def _hg_block_decay(b, rev):
    n_blk = HG_CHUNK // HG_BLOCK
    edge = [b[i * HG_BLOCK:i * HG_BLOCK + 1] if rev else b[(i + 1) * HG_BLOCK - 1:(i + 1) * HG_BLOCK]
            for i in range(n_blk)]
    worst = edge[-1] if rev else edge[0]
    for i in range(1, n_blk):
        worst = jnp.minimum(worst, edge[i - 1] - edge[i] if rev else edge[i] - edge[i - 1])
    return worst


def _hg_chunks(jobs, lvl_masks, blocked):
    c = HG_CHUNK
    for j in jobs:
        q, k, b = j["q"], j["k"], j["b"]
        b_tot = b[0:1] if j["rev"] else b[c - 1:c]
        j["dec"] = jnp.exp2(b_tot)
        j["q_in"] = (q * jnp.exp2(b)).astype(BF16)
        k_out = (k * jnp.exp2(b_tot - b)).astype(BF16)
        j["st_add"] = _dot(j["v"].T.astype(BF16), k_out)
        j["p"] = None

    for lvl, half in enumerate(_hg_levels(HG_BLOCK if blocked else HG_LEAF)):
        n_grp = c // (2 * half)
        shape3 = (n_grp, 2 * half, HG_D)
        zero = jnp.zeros((n_grp, half, HG_D), F32)
        for j in jobs:
            b3, q3, k3 = j["b"].reshape(shape3), j["q"].reshape(shape3), j["k"].reshape(shape3)
            if j["rev"]:
                mid = b3[:, half:half + 1]
                q_l = jnp.concatenate([q3[:, :half] * jnp.exp2(b3[:, :half] - mid), zero], axis=1)
                k_l = jnp.concatenate([zero, k3[:, half:] * jnp.exp2(mid - b3[:, half:])], axis=1)
            else:
                mid = b3[:, half - 1:half]
                q_l = jnp.concatenate([zero, q3[:, half:] * jnp.exp2(b3[:, half:] - mid)], axis=1)
                k_l = jnp.concatenate([k3[:, :half] * jnp.exp2(mid - b3[:, :half]), zero], axis=1)
            p_l = _dot_nt(q_l.reshape(c, HG_D).astype(BF16), k_l.reshape(c, HG_D).astype(BF16))
            p_l = p_l * lvl_masks[lvl] if n_grp > 1 else p_l
            j["p"] = p_l if j["p"] is None else j["p"] + p_l

    if blocked:
        n_blk = c // HG_BLOCK
        shape3 = (n_blk, HG_BLOCK, HG_D)
        zero_row = jnp.zeros((1, 1, HG_D), F32)
        for j in jobs:
            b3 = j["b"].reshape(shape3)
            if j["rev"]:
                ref = jnp.concatenate([b3[1:, 0:1], zero_row], axis=0)
            else:
                ref = jnp.concatenate([zero_row, b3[:-1, HG_BLOCK - 1:HG_BLOCK]], axis=0)
            b_loc = b3 - ref
            q_b = (j["q"].reshape(shape3) * jnp.exp2(b_loc)).reshape(c, HG_D).astype(BF16)
            k_b = (j["k"].reshape(shape3) * jnp.exp2(-b_loc)).reshape(c, HG_D).astype(BF16)
            j["p_leaf"] = _dot_nt(q_b, k_b) * j["block_mask"]
    else:
        for j in jobs:
            j["kpad"][HG_LEAF:HG_LEAF + c, :] = j["k"]
            j["bpad"][HG_LEAF:HG_LEAF + c, :] = j["b"]
        for j in jobs:
            w = [(j["q"] * j["k"]).astype(BF16)]
            for d in range(1, HG_LEAF):
                lo = HG_LEAF + d if j["rev"] else HG_LEAF - d
                k_s = j["kpad"][lo:lo + c, :]
                b_s = j["bpad"][lo:lo + c, :]
                w.append((j["q"] * k_s * jnp.exp2(j["b"] - b_s)).astype(BF16))
            j["row_sums"] = _dot(jnp.concatenate(w, axis=1), j["spread"])
        for j in jobs:
            j["p_leaf"] = (pltpu.roll(j["row_sums"], 0, 1, stride=1, stride_axis=0)
                           * j["leaf_mask"])

    outs = []
    for j in jobs:
        p = (j["p"] + j["p_leaf"]).astype(BF16)
        outs.append((_dot(p, j["v"].astype(BF16)), j["q_in"], j["st_add"], j["dec"]))
    return outs


def _hgrn_kernel(q_ref, i_ref, g_ref, ff_ref, fb_ref, lbp_ref, gain_ref, tri_ref, lvl_ref, leaf_ref,
                 block_ref, spread_ref, out_ref, k_ref, b_ref, o_ref, qin_ref, add_ref, dec_ref,
                 st_ref, kpad_ref, bpad_ref):
    s_len = q_ref.shape[0]
    n_chunk = s_len // HG_CHUNK
    logit_refs = (ff_ref, fb_ref)
    chunk_rows = lambda ci: pl.ds(pl.multiple_of(ci * HG_CHUNK, HG_CHUNK), HG_CHUNK)

    def lower_bound(d):
        p = lbp_ref[d]
        e = jnp.exp(p - jnp.max(p, axis=0, keepdims=True))
        return e[0:1] / jnp.sum(e, axis=0, keepdims=True)

    lbs = (lower_bound(0), lower_bound(1))

    def gates(n, worst):
        for u in range(HG_UNROLL):
            rows = chunk_rows(n * HG_UNROLL + u)
            for d in range(2):
                k, b = _hg_gates(logit_refs[d][rows, :], lbs[d], tri_ref[d])
                k_ref[d, rows, :] = k
                b_ref[d, rows, :] = b
                worst = jnp.minimum(worst, _hg_block_decay(b, bool(d)))
        return worst

    worst = lax.fori_loop(0, n_chunk // HG_UNROLL, gates, jnp.zeros((1, HG_D), F32))
    blocks_representable = jnp.min(worst) >= -HG_MAX_BLOCK_DECAY

    def within_chunks(blocked):
        def body(n, carry):
            lvl_masks = [lvl_ref[l] for l in range(lvl_ref.shape[0])]
            jobs = []
            for u in range(HG_UNROLL):
                ci = n * HG_UNROLL + u
                rows = chunk_rows(ci)
                for d, rev in enumerate((False, True)):
                    slot = d * HG_UNROLL + u
                    jobs.append(dict(
                        ci=ci, rows=rows, d=d, rev=rev, q=q_ref[rows, :].astype(F32),
                        k=k_ref[d, rows, :], b=b_ref[d, rows, :], v=i_ref[rows, :].astype(F32),
                        kpad=kpad_ref.at[slot], bpad=bpad_ref.at[slot], leaf_mask=leaf_ref[d],
                        block_mask=block_ref[d], spread=spread_ref[d]))
            for j, (o, q_in, st_add, dec) in zip(jobs, _hg_chunks(jobs, lvl_masks, blocked)):
                o_ref[j["d"], j["rows"], :] = o
                qin_ref[j["rows"], j["d"] * HG_D:(j["d"] + 1) * HG_D] = q_in
                add_ref[j["d"], j["ci"]] = st_add
                dec_ref[j["d"], j["ci"]] = dec
            return carry
        return body

    @pl.when(blocks_representable)
    def _():
        lax.fori_loop(0, n_chunk // HG_UNROLL, within_chunks(True), 0)

    @pl.when(jnp.logical_not(blocks_representable))
    def _():
        kpad_ref[...] = jnp.zeros_like(kpad_ref)
        bpad_ref[...] = jnp.zeros_like(bpad_ref)
        lax.fori_loop(0, n_chunk // HG_UNROLL, within_chunks(False), 0)

    def across_chunks(n, states):
        new_states = []
        for d, st in enumerate(states):
            ci = (n_chunk - 1 - n) if d else n
            st_ref[ci, :, d * HG_D:(d + 1) * HG_D] = st.astype(BF16)
            new_states.append(dec_ref[d, ci] * st + add_ref[d, ci])
        return tuple(new_states)

    zero_state = jnp.zeros((HG_D, HG_D), F32)
    lax.fori_loop(0, n_chunk, across_chunks, (zero_state, zero_state))

    def finish(n, carry):
        rows = chunk_rows(n)
        o = o_ref[0, rows, :] + o_ref[1, rows, :] + _dot_nt(qin_ref[rows, :], st_ref[n])
        y = _rms(o, gain_ref[...])
        gate = g_ref[rows, :].astype(F32)
        out_ref[rows, :] = (y * gate * _sigmoid(gate)).astype(BF16)
        return carry

    lax.fori_loop(0, n_chunk, finish, 0, unroll=4)


def _hgrn_stage(qig, fgate, lb_param, gain, batch, s_len):
    assert HG_CHUNK == LANES, "the leaf path rotates each row of a (C, C) tile by its row index"
    t_tok = qig.shape[0]
    c = HG_CHUNK
    r = jnp.arange(c)
    t_i, s_i = r[:, None], r[None, :]
    lower = s_i <= t_i
    causal = jnp.stack([lower, lower.T])
    tri = jnp.tile(causal, (1, 1, 3)).astype(BF16)
    lvl_masks = jnp.stack([t_i // (2 * h) == s_i // (2 * h) for h in _hg_levels(HG_LEAF)]).astype(F32)
    leaf_mask = (causal & (t_i // HG_LEAF == s_i // HG_LEAF)).astype(F32)
    block_mask = (causal & (t_i // HG_BLOCK == s_i // HG_BLOCK)).astype(F32)
    dist = jnp.repeat(jnp.arange(HG_LEAF), HG_D)[:, None]
    spread = jnp.stack([(-s_i) % HG_LEAF == dist, s_i % HG_LEAF == dist]).astype(BF16)
    blk = lambda col: pl.BlockSpec((s_len, HG_D), lambda b, h, col=col: (b, col * HG_HEADS + h))
    consts = (tri, lvl_masks, leaf_mask, block_mask, spread)
    seq = lambda dtype: pltpu.VMEM((2, s_len, HG_D), dtype)
    return pl.pallas_call(
        _hgrn_kernel,
        grid=(batch, HG_HEADS),
        in_specs=[blk(0), blk(1), blk(2), blk(0), blk(1),
                  pl.BlockSpec((2, 2, HG_D), lambda b, h: (0, 0, h)),
                  pl.BlockSpec((1, HG_D), lambda b, h: (0, h))] + [_const_spec(t.shape) for t in consts],
        out_specs=pl.BlockSpec((s_len, HG_D), lambda b, h: (b, h)),
        out_shape=jax.ShapeDtypeStruct((t_tok, HG_W), BF16),
        scratch_shapes=[seq(F32), seq(F32), seq(F32), pltpu.VMEM((s_len, 2 * HG_D), BF16),
                        pltpu.VMEM((2, s_len // c, HG_D, HG_D), F32),
                        pltpu.VMEM((2, s_len // c, 1, HG_D), F32),
                        pltpu.VMEM((s_len // c, HG_D, 2 * HG_D), BF16),
                        pltpu.VMEM((2 * HG_UNROLL, c + 2 * HG_LEAF, HG_D), F32),
                        pltpu.VMEM((2 * HG_UNROLL, c + 2 * HG_LEAF, HG_D), F32)],
        compiler_params=pltpu.CompilerParams(dimension_semantics=("arbitrary", "arbitrary"),
                                             vmem_limit_bytes=VMEM_LIMIT),
        name="hgrn_stage",
    )(qig, qig, qig, fgate, fgate, lb_param, gain, *consts)


def _attn_kernel(q_ref, k_ref, v_ref, o_ref):
    n_heads, s_len, _ = q_ref.shape
    sub = min(ATTN_SUB, s_len)
    items = [(h, slice(i * sub, (i + 1) * sub)) for h in range(n_heads) for i in range(s_len // sub)]
    scores = lambda h, rows: _dot_nt(q_ref[h, rows, :], k_ref[h])
    pending = [scores(*item) for item in items[:ATTN_DEPTH]]
    for n, (h, rows) in enumerate(items):
        s = pending.pop(0)
        if n + ATTN_DEPTH < len(items):
            pending.append(scores(*items[n + ATTN_DEPTH]))
        cols = slice(h * MLA_V, (h + 1) * MLA_V)
        p = jnp.exp2(s - jnp.max(s, axis=-1, keepdims=True))
        o = _dot(p.astype(BF16), v_ref[:, cols])
        o_ref[rows, cols] = (o * (1.0 / jnp.sum(p, axis=-1, keepdims=True))).astype(BF16)


def _attn_stage(q_all, k_all, v, batch, s_len):
    t_tok = v.shape[0]
    n_grp = MLA_HEADS // ATTN_HEADS
    return pl.pallas_call(
        _attn_kernel,
        grid=(batch, n_grp),
        in_specs=[pl.BlockSpec((ATTN_HEADS, s_len, QK_PAD), lambda b, g: (g, b, 0)),
                  pl.BlockSpec((ATTN_HEADS, s_len, QK_PAD), lambda b, g: (g, b, 0)),
                  pl.BlockSpec((s_len, ATTN_HEADS * MLA_V), lambda b, g: (b, g))],
        out_specs=pl.BlockSpec((s_len, ATTN_HEADS * MLA_V), lambda b, g: (b, g)),
        out_shape=jax.ShapeDtypeStruct((t_tok, MLA_HEADS * MLA_V), BF16),
        compiler_params=pltpu.CompilerParams(dimension_semantics=("arbitrary",) * 2,
                                             vmem_limit_bytes=VMEM_LIMIT),
        name="attn_stage",
    )(q_all, k_all, v)


def _output_kernel(x_ref, a_ref, ob_ref, g_mla_ref, w_oa_ref, w_ob_ref, g_ffn_ref, w_up_ref,
                   w_dn_ref, out_ref):
    ob = _rms(ob_ref[...].astype(F32), g_mla_ref[...]).astype(BF16)
    x1 = x_ref[...] + _dot(a_ref[...], w_oa_ref[...]) + _dot(ob, w_ob_ref[...])
    h = _rms(x1, g_ffn_ref[...]).astype(BF16)
    up = jnp.maximum(_dot(h, w_up_ref[...]), 0.0)
    out_ref[...] = x1 + _dot((up * up).astype(BF16), w_dn_ref[...])


def _output_stage(x2, a, ob, g_mla, w_oa, w_ob, g_ffn, w_up, w_dn, tm):
    t_tok, d_model = x2.shape
    consts = (g_mla, w_oa, w_ob, g_ffn, w_up, w_dn)
    return pl.pallas_call(
        _output_kernel,
        grid=(t_tok // tm,),
        in_specs=[pl.BlockSpec((tm, d_model), lambda i: (i, 0)),
                  pl.BlockSpec((tm, HG_W), lambda i: (i, 0)),
                  pl.BlockSpec((tm, MLA_HEADS * MLA_V), lambda i: (i, 0))]
                 + [_const_spec(c.shape) for c in consts],
        out_specs=pl.BlockSpec((tm, d_model), lambda i: (i, 0)),
        out_shape=jax.ShapeDtypeStruct((t_tok, d_model), F32),
        compiler_params=pltpu.CompilerParams(dimension_semantics=("arbitrary",),
                                             vmem_limit_bytes=VMEM_LIMIT),
        name="output_stage",
    )(x2, a, ob, *consts)


def kernel(x, positions, g_mix_norm, w_in, lb_param, g_hgrn_out, g_cq, w_q_up, g_ckv, w_kv_up,
           g_q_norm, g_k_norm, g_mla_out, w_out, g_ffn_norm, w_up, w_down):
    batch, s_len, d_model = x.shape
    assert g_mix_norm.shape[0] == 1 and lb_param.shape[1] == 2, "one layer"
    assert s_len % (HG_CHUNK * HG_UNROLL) == 0
    t_tok = batch * s_len
    row = lambda t: t.reshape(1, -1).astype(F32)

    w = w_in[0]
    o_q, o_ff, o_fb, o_i, o_g, o_cq = (HG_W * n for n in range(6))
    o_ckv = o_cq + Q_LORA
    o_kr = o_ckv + KV_LORA
    w_qig = jnp.concatenate([w[:, o_q:o_ff], w[:, o_i:o_g], w[:, o_g:o_cq]], axis=1).astype(BF16)
    w_f = w[:, o_ff:o_i].astype(BF16)
    half = MLA_ROPE // 2
    swap = lambda t: jnp.concatenate([t[..., half:], t[..., :half]], axis=-1)
    pad_lanes = lambda t: jnp.pad(t, [(0, 0)] * (t.ndim - 1) + [(0, LANES - MLA_ROPE)])
    w_kr = w[:, o_kr:]
    w_mla = jnp.concatenate([w[:, o_cq:o_kr], pad_lanes(w_kr), pad_lanes(swap(w_kr))],
                            axis=1).astype(BF16)

    wq = w_q_up[0].reshape(Q_LORA, MLA_HEADS, MLA_QK)
    wq_rope = wq[:, :, MLA_NOPE:]
    w_qup = jnp.concatenate([wq[:, :, :MLA_NOPE].reshape(Q_LORA, -1),
                             pad_lanes(wq_rope).reshape(Q_LORA, -1),
                             pad_lanes(swap(wq_rope)).reshape(Q_LORA, -1)], axis=1).astype(BF16)
    wkv = w_kv_up[0].reshape(KV_LORA, MLA_HEADS, MLA_NOPE + MLA_V)
    w_kvup = jnp.concatenate([wkv[:, :, :MLA_NOPE].reshape(KV_LORA, -1),
                              wkv[:, :, MLA_NOPE:].reshape(KV_LORA, -1)], axis=1).astype(BF16)
    rope_gain = lambda t: jnp.stack([pad_lanes(t), pad_lanes(swap(t))]).astype(F32)
    g_qn, g_qr = row(g_q_norm[0, :MLA_NOPE]), rope_gain(g_q_norm[0, MLA_NOPE:])
    g_kn, g_kr = row(g_k_norm[0, :MLA_NOPE]), rope_gain(g_k_norm[0, MLA_NOPE:])
    invf = (ROPE_THETA ** (-jnp.arange(0, MLA_ROPE, 2, dtype=F32) / MLA_ROPE)).reshape(-1, 1)

    x2 = x.reshape(t_tok, d_model)
    pos = positions.reshape(1, t_tok).astype(F32)
    tm = min(512, s_len)
    qig, fgate, q_all, k_all, v = _input_stage(
        x2, pos, invf, row(g_mix_norm[0]), w_qig, w_f, w_mla, row(g_cq[0]), w_qup, row(g_ckv[0]),
        w_kvup, g_qn, g_qr, g_kn, g_kr, tm)

    a = _hgrn_stage(qig, fgate, lb_param.astype(F32), g_hgrn_out[0].reshape(1, HG_W).astype(F32),
                    batch, s_len)
    ob = _attn_stage(q_all, k_all, v, batch, s_len)

    wo = w_out[0].astype(BF16)
    out = _output_stage(x2, a, ob, row(g_mla_out[0]), wo[:HG_W], wo[HG_W:], row(g_ffn_norm[0]),
                        w_up[0].astype(BF16), w_down[0].astype(BF16), tm)
    return out.reshape(batch, s_len, d_model)
```

```python
import functools

import jax
import jax.numpy as jnp
from jax import lax
from jax.experimental import pallas as pl
from jax.experimental.pallas import tpu as pltpu

NORM_EPS = 1e-6
HG_HEADS = 4
HG_D = 128
HG_W = HG_HEADS * HG_D
MLA_HEADS = 4
MLA_NOPE = 128
MLA_ROPE = 64
MLA_V = 128
MLA_QK = MLA_NOPE + MLA_ROPE
Q_LORA = 384
KV_LORA = 256
ROPE_THETA = 10000.0
LANES = 128
QK_PAD = 2 * LANES

HG_CHUNK = 128
HG_LEAF = 8
HG_BLOCK = 32
HG_MAX_BLOCK_DECAY = 100.0
HG_UNROLL = 8
ATTN_SUB = 512
ATTN_HEADS = 2
ATTN_DEPTH = 1
INPUT_SUB = 256
LOG2_E = 1.4426950408889634
VMEM_LIMIT = 56 * 1024 * 1024

F32 = jnp.float32
BF16 = jnp.bfloat16


def _dot(a, b):
    return jnp.dot(a, b, preferred_element_type=F32)


def _dot_nt(a, b):
    return lax.dot_general(a, b, (((1,), (1,)), ((), ())), preferred_element_type=F32)


def _rms(t, gain):
    return t * lax.rsqrt(jnp.mean(t * t, axis=-1, keepdims=True) + NORM_EPS) * gain


def _sigmoid(t):
    return 1.0 / (1.0 + jnp.exp2(t * -LOG2_E))


def _store_slabs(ref, rows, t):
    for j in range(ref.shape[0]):
        ref[j, rows, :] = t[:, j * LANES:(j + 1) * LANES].astype(ref.dtype)


def _load_slabs(ref):
    return jnp.concatenate([ref[j] for j in range(ref.shape[0])], axis=1)


def _input_kernel(x_ref, pos_ref, invf_ref, g_mix_ref, w_qig_ref, w_f_ref, w_mla_ref, g_cq_ref,
                  w_qup_ref, g_ckv_ref, w_kvup_ref, g_qn_ref, g_qr_ref, g_kn_ref, g_kr_ref,
                  qig_ref, f_ref, qa_ref, ka_ref, v_ref):
    tm = x_ref.shape[0]
    sub = min(INPUT_SUB, tm)
    tiles = [dict(rows=slice(i * sub, (i + 1) * sub)) for i in range(tm // sub)]
    scale = MLA_QK ** -0.5 * LOG2_E

    def project(t):
        h = _rms(x_ref[t["rows"], :], g_mix_ref[...]).astype(BF16)
        _store_slabs(qig_ref, t["rows"], _dot(h, w_qig_ref[...]))
        _store_slabs(f_ref, t["rows"], _dot(h, w_f_ref[...]))
        t["m"] = _dot(h, w_mla_ref[...])

    def up_project(t):
        c_q = t["m"][:, :Q_LORA]
        c_kv = t["m"][:, Q_LORA:Q_LORA + KV_LORA]
        t["q"] = _dot(_rms(c_q, g_cq_ref[...]).astype(BF16), w_qup_ref[...])
        t["kv"] = _dot(_rms(c_kv, g_ckv_ref[...]).astype(BF16), w_kvup_ref[...])

    def rope_tables():
        ang = invf_ref[...] * pos_ref[...]
        c, s = jnp.cos(ang), jnp.sin(ang)
        z = jnp.zeros_like(c)
        return jnp.concatenate([c, c, z, z], axis=0).T, jnp.concatenate([-s, s, z, z], axis=0).T

    def heads(t, cos_t, sin_t):
        rows, q, kv = t["rows"], t["q"], t["kv"]
        _store_slabs(v_ref, rows, kv[:, MLA_HEADS * MLA_NOPE:])
        cos_r, sin_r = cos_t[rows, :], sin_t[rows, :]
        q_cos, q_sin = g_qr_ref[0:1] * cos_r, g_qr_ref[1:2] * sin_r
        rope_cols = Q_LORA + KV_LORA
        k_r = t["m"][:, rope_cols:rope_cols + LANES]
        k_pe = (k_r * (g_kr_ref[0:1] * cos_r)
                + t["m"][:, rope_cols + LANES:] * (g_kr_ref[1:2] * sin_r))
        sq_kr = k_r * k_r
        for hd in range(MLA_HEADS):
            lo, hi = hd * LANES, (hd + 1) * LANES
            q_n = q[:, lo:hi]
            q_r = q[:, MLA_HEADS * MLA_NOPE + lo:MLA_HEADS * MLA_NOPE + hi]
            q_r_swapped = q[:, 2 * MLA_HEADS * MLA_NOPE + lo:2 * MLA_HEADS * MLA_NOPE + hi]
            ssq = jnp.sum(q_n * q_n + q_r * q_r, axis=-1, keepdims=True)
            rstd = lax.rsqrt(ssq * (1.0 / MLA_QK) + NORM_EPS) * scale
            q_pe = q_r * q_cos + q_r_swapped * q_sin
            qa_ref[hd, rows, :] = jnp.concatenate([q_n * g_qn_ref[...] * rstd, q_pe * rstd],
                                                  axis=1).astype(BF16)
            k_n = kv[:, lo:hi]
            ssq = jnp.sum(k_n * k_n + sq_kr, axis=-1, keepdims=True)
            rstd = lax.rsqrt(ssq * (1.0 / MLA_QK) + NORM_EPS)
            ka_ref[hd, rows, :] = jnp.concatenate([k_n * g_kn_ref[...] * rstd, k_pe * rstd],
                                                  axis=1).astype(BF16)

    tables = None
    for i, t in enumerate(tiles):
        project(t)
        if i == 0:
            tables = rope_tables()
        else:
            heads(tiles[i - 1], *tables)
        up_project(t)
    heads(tiles[-1], *tables)


def _const_spec(shape):
    nd = len(shape)
    return pl.BlockSpec(shape, lambda *_: (0,) * nd, pipeline_mode=pl.Buffered(1))


def _input_stage(x2, pos, invf, g_mix, w_qig, w_f, w_mla, g_cq, w_qup, g_ckv, w_kvup,
                 g_qn, g_qr, g_kn, g_kr, tm):
    t_tok, d_model = x2.shape
    consts = (invf, g_mix, w_qig, w_f, w_mla, g_cq, w_qup, g_ckv, w_kvup, g_qn, g_qr, g_kn, g_kr)
    return pl.pallas_call(
        _input_kernel,
        grid=(t_tok // tm,),
        in_specs=[pl.BlockSpec((tm, d_model), lambda i: (i, 0)),
                  pl.BlockSpec((1, tm), lambda i: (0, i))] + [_const_spec(c.shape) for c in consts],
        out_specs=[pl.BlockSpec((3 * HG_HEADS, tm, HG_D), lambda i: (0, i, 0)),
                   pl.BlockSpec((2 * HG_HEADS, tm, HG_D), lambda i: (0, i, 0)),
                   pl.BlockSpec((MLA_HEADS, tm, QK_PAD), lambda i: (0, i, 0)),
                   pl.BlockSpec((MLA_HEADS, tm, QK_PAD), lambda i: (0, i, 0)),
                   pl.BlockSpec((MLA_HEADS, tm, MLA_V), lambda i: (0, i, 0))],
        out_shape=[jax.ShapeDtypeStruct((3 * HG_HEADS, t_tok, HG_D), BF16),
                   jax.ShapeDtypeStruct((2 * HG_HEADS, t_tok, HG_D), F32),
                   jax.ShapeDtypeStruct((MLA_HEADS, t_tok, QK_PAD), BF16),
                   jax.ShapeDtypeStruct((MLA_HEADS, t_tok, QK_PAD), BF16),
                   jax.ShapeDtypeStruct((MLA_HEADS, t_tok, MLA_V), BF16)],
        compiler_params=pltpu.CompilerParams(dimension_semantics=("arbitrary",),
                                             vmem_limit_bytes=VMEM_LIMIT),
        name="input_stage",
    )(x2, pos, *consts)


def _hg_levels(leaf):
    sizes = []
    half = HG_CHUNK // 2
    while half >= leaf:
        sizes.append(half)
        half //= 2
    return tuple(sizes)


def _hg_gates(logit, lb, tri):
    f = lb + (1.0 - lb) * _sigmoid(logit)
    k = 1.0 - f
    g = jnp.log2(f)
    g_hi = g.astype(BF16)
    r1 = g - g_hi.astype(F32)
    g_mid = r1.astype(BF16)
    g_lo = (r1 - g_mid.astype(F32)).astype(BF16)
    return k, _dot(tri, jnp.concatenate([g_hi, g_mid, g_lo], axis=0))


def _hg_block_decay(b, rev):
    n_blk = HG_CHUNK // HG_BLOCK
    edge = [b[i * HG_BLOCK:i * HG_BLOCK + 1] if rev else b[(i + 1) * HG_BLOCK - 1:(i + 1) * HG_BLOCK]
            for i in range(n_blk)]
    worst = edge[-1] if rev else edge[0]
    for i in range(1, n_blk):
        worst = jnp.minimum(worst, edge[i - 1] - edge[i] if rev else edge[i] - edge[i - 1])
    return worst


def _hg_chunks(jobs, lvl_masks, blocked):
    c = HG_CHUNK
    for j in jobs:
        q, k, b = j["q"], j["k"], j["b"]
        b_tot = b[0:1] if j["rev"] else b[c - 1:c]
        j["dec"] = jnp.exp2(b_tot)
        j["q_in"] = (q * jnp.exp2(b)).astype(BF16)
        k_out = (k * jnp.exp2(b_tot - b)).astype(BF16)
        j["st_add"] = _dot(j["v"].T.astype(BF16), k_out)
        j["p"] = None

    for lvl, half in enumerate(_hg_levels(HG_BLOCK if blocked else HG_LEAF)):
        n_grp = c // (2 * half)
        shape3 = (n_grp, 2 * half, HG_D)
        zero = jnp.zeros((n_grp, half, HG_D), F32)
        for j in jobs:
            b3, q3, k3 = j["b"].reshape(shape3), j["q"].reshape(shape3), j["k"].reshape(shape3)
            if j["rev"]:
                mid = b3[:, half:half + 1]
                q_l = jnp.concatenate([q3[:, :half] * jnp.exp2(b3[:, :half] - mid), zero], axis=1)
                k_l = jnp.concatenate([zero, k3[:, half:] * jnp.exp2(mid - b3[:, half:])], axis=1)
            else:
                mid = b3[:, half - 1:half]
                q_l = jnp.concatenate([zero, q3[:, half:] * jnp.exp2(b3[:, half:] - mid)], axis=1)
                k_l = jnp.concatenate([k3[:, :half] * jnp.exp2(mid - b3[:, :half]), zero], axis=1)
            p_l = _dot_nt(q_l.reshape(c, HG_D).astype(BF16), k_l.reshape(c, HG_D).astype(BF16))
            p_l = p_l * lvl_masks[lvl] if n_grp > 1 else p_l
            j["p"] = p_l if j["p"] is None else j["p"] + p_l

    if blocked:
        n_blk = c // HG_BLOCK
        shape3 = (n_blk, HG_BLOCK, HG_D)
        zero_row = jnp.zeros((1, 1, HG_D), F32)
        for j in jobs:
            b3 = j["b"].reshape(shape3)
            if j["rev"]:
                ref = jnp.concatenate([b3[1:, 0:1], zero_row], axis=0)
            else:
                ref = jnp.concatenate([zero_row, b3[:-1, HG_BLOCK - 1:HG_BLOCK]], axis=0)
            b_loc = b3 - ref
            q_b = (j["q"].reshape(shape3) * jnp.exp2(b_loc)).reshape(c, HG_D).astype(BF16)
            k_b = (j["k"].reshape(shape3) * jnp.exp2(-b_loc)).reshape(c, HG_D).astype(BF16)
            j["p_leaf"] = _dot_nt(q_b, k_b) * j["block_mask"]
    else:
        for j in jobs:
            j["kpad"][HG_LEAF:HG_LEAF + c, :] = j["k"]
            j["bpad"][HG_LEAF:HG_LEAF + c, :] = j["b"]
        for j in jobs:
            w = [(j["q"] * j["k"]).astype(BF16)]
            for d in range(1, HG_LEAF):
                lo = HG_LEAF + d if j["rev"] else HG_LEAF - d
                k_s = j["kpad"][lo:lo + c, :]
                b_s = j["bpad"][lo:lo + c, :]
                w.append((j["q"] * k_s * jnp.exp2(j["b"] - b_s)).astype(BF16))
            j["row_sums"] = _dot(jnp.concatenate(w, axis=1), j["spread"])
        for j in jobs:
            j["p_leaf"] = (pltpu.roll(j["row_sums"], 0, 1, stride=1, stride_axis=0)
                           * j["leaf_mask"])

    outs = []
    for j in jobs:
        p = (j["p"] + j["p_leaf"]).astype(BF16)
        outs.append((_dot(p, j["v"].astype(BF16)), j["q_in"], j["st_add"], j["dec"]))
    return outs


def _hgrn_kernel(q_ref, i_ref, g_ref, ff_ref, fb_ref, lbp_ref, gain_ref, tri_ref, lvl_ref, leaf_ref,
                 block_ref, spread_ref, out_ref, k_ref, b_ref, o_ref, qin_ref, add_ref, dec_ref,
                 st_ref, kpad_ref, bpad_ref):
    s_len = q_ref.shape[0]
    n_chunk = s_len // HG_CHUNK
    logit_refs = (ff_ref, fb_ref)
    chunk_rows = lambda ci: pl.ds(pl.multiple_of(ci * HG_CHUNK, HG_CHUNK), HG_CHUNK)

    def lower_bound(d):
        p = lbp_ref[d]
        e = jnp.exp(p - jnp.max(p, axis=0, keepdims=True))
        return e[0:1] / jnp.sum(e, axis=0, keepdims=True)

    lbs = (lower_bound(0), lower_bound(1))

    def gates(n, worst):
        for u in range(HG_UNROLL):
            rows = chunk_rows(n * HG_UNROLL + u)
            for d in range(2):
                k, b = _hg_gates(logit_refs[d][rows, :], lbs[d], tri_ref[d])
                k_ref[d, rows, :] = k
                b_ref[d, rows, :] = b
                worst = jnp.minimum(worst, _hg_block_decay(b, bool(d)))
        return worst

    worst = lax.fori_loop(0, n_chunk // HG_UNROLL, gates, jnp.zeros((1, HG_D), F32))
    blocks_representable = jnp.min(worst) >= -HG_MAX_BLOCK_DECAY

    def within_chunks(blocked):
        def body(n, carry):
            lvl_masks = [lvl_ref[l] for l in range(lvl_ref.shape[0])]
            jobs = []
            for u in range(HG_UNROLL):
                ci = n * HG_UNROLL + u
                rows = chunk_rows(ci)
                for d, rev in enumerate((False, True)):
                    slot = d * HG_UNROLL + u
                    jobs.append(dict(
                        ci=ci, rows=rows, d=d, rev=rev, q=q_ref[rows, :].astype(F32),
                        k=k_ref[d, rows, :], b=b_ref[d, rows, :], v=i_ref[rows, :].astype(F32),
                        kpad=kpad_ref.at[slot], bpad=bpad_ref.at[slot], leaf_mask=leaf_ref[d],
                        block_mask=block_ref[d], spread=spread_ref[d]))
            for j, (o, q_in, st_add, dec) in zip(jobs, _hg_chunks(jobs, lvl_masks, blocked)):
                o_ref[j["d"], j["rows"], :] = o
                qin_ref[j["rows"], j["d"] * HG_D:(j["d"] + 1) * HG_D] = q_in
                add_ref[j["d"], j["ci"]] = st_add
                dec_ref[j["d"], j["ci"]] = dec
            return carry
        return body

    @pl.when(blocks_representable)
    def _():
        lax.fori_loop(0, n_chunk // HG_UNROLL, within_chunks(True), 0)

    @pl.when(jnp.logical_not(blocks_representable))
    def _():
        kpad_ref[...] = jnp.zeros_like(kpad_ref)
        bpad_ref[...] = jnp.zeros_like(bpad_ref)
        lax.fori_loop(0, n_chunk // HG_UNROLL, within_chunks(False), 0)

    def across_chunks(n, states):
        new_states = []
        for d, st in enumerate(states):
            ci = (n_chunk - 1 - n) if d else n
            st_ref[ci, :, d * HG_D:(d + 1) * HG_D] = st.astype(BF16)
            new_states.append(dec_ref[d, ci] * st + add_ref[d, ci])
        return tuple(new_states)

    zero_state = jnp.zeros((HG_D, HG_D), F32)
    lax.fori_loop(0, n_chunk, across_chunks, (zero_state, zero_state))

    def finish(n, carry):
        rows = chunk_rows(n)
        o = o_ref[0, rows, :] + o_ref[1, rows, :] + _dot_nt(qin_ref[rows, :], st_ref[n])
        y = _rms(o, gain_ref[...])
        gate = g_ref[rows, :].astype(F32)
        out_ref[rows, :] = (y * gate * _sigmoid(gate)).astype(BF16)
        return carry

    lax.fori_loop(0, n_chunk, finish, 0, unroll=4)


def _hgrn_stage(qig, fgate, lb_param, gain, batch, s_len):
    assert HG_CHUNK == LANES, "the leaf path rotates each row of a (C, C) tile by its row index"
    t_tok = qig.shape[1]
    c = HG_CHUNK
    r = jnp.arange(c)
    t_i, s_i = r[:, None], r[None, :]
    lower = s_i <= t_i
    causal = jnp.stack([lower, lower.T])
    tri = jnp.tile(causal, (1, 1, 3)).astype(BF16)
    lvl_masks = jnp.stack([t_i // (2 * h) == s_i // (2 * h) for h in _hg_levels(HG_LEAF)]).astype(F32)
    leaf_mask = (causal & (t_i // HG_LEAF == s_i // HG_LEAF)).astype(F32)
    block_mask = (causal & (t_i // HG_BLOCK == s_i // HG_BLOCK)).astype(F32)
    dist = jnp.repeat(jnp.arange(HG_LEAF), HG_D)[:, None]
    spread = jnp.stack([(-s_i) % HG_LEAF == dist, s_i % HG_LEAF == dist]).astype(BF16)
    blk = lambda col: pl.BlockSpec((None, s_len, HG_D),
                                   lambda b, h, col=col: (col * HG_HEADS + h, b, 0))
    consts = (tri, lvl_masks, leaf_mask, block_mask, spread)
    seq = lambda dtype: pltpu.VMEM((2, s_len, HG_D), dtype)
    return pl.pallas_call(
        _hgrn_kernel,
        grid=(batch, HG_HEADS),
        in_specs=[blk(0), blk(1), blk(2), blk(0), blk(1),
                  pl.BlockSpec((2, 2, HG_D), lambda b, h: (0, 0, h)),
                  pl.BlockSpec((1, HG_D), lambda b, h: (0, h))] + [_const_spec(t.shape) for t in consts],
        out_specs=pl.BlockSpec((None, s_len, HG_D), lambda b, h: (h, b, 0)),
        out_shape=jax.ShapeDtypeStruct((HG_HEADS, t_tok, HG_D), BF16),
        scratch_shapes=[seq(F32), seq(F32), seq(F32), pltpu.VMEM((s_len, 2 * HG_D), BF16),
                        pltpu.VMEM((2, s_len // c, HG_D, HG_D), F32),
                        pltpu.VMEM((2, s_len // c, 1, HG_D), F32),
                        pltpu.VMEM((s_len // c, HG_D, 2 * HG_D), BF16),
                        pltpu.VMEM((2 * HG_UNROLL, c + 2 * HG_LEAF, HG_D), F32),
                        pltpu.VMEM((2 * HG_UNROLL, c + 2 * HG_LEAF, HG_D), F32)],
        compiler_params=pltpu.CompilerParams(dimension_semantics=("arbitrary", "arbitrary"),
                                             vmem_limit_bytes=VMEM_LIMIT),
        name="hgrn_stage",
    )(qig, qig, qig, fgate, fgate, lb_param, gain, *consts)


def _attn_kernel(q_ref, k_ref, v_ref, o_ref):
    n_heads, s_len, _ = q_ref.shape
    sub = min(ATTN_SUB, s_len)
    items = [(h, slice(i * sub, (i + 1) * sub)) for h in range(n_heads) for i in range(s_len // sub)]
    scores = lambda h, rows: _dot_nt(q_ref[h, rows, :], k_ref[h])
    pending = [scores(*item) for item in items[:ATTN_DEPTH]]
    for n, (h, rows) in enumerate(items):
        s = pending.pop(0)
        if n + ATTN_DEPTH < len(items):
            pending.append(scores(*items[n + ATTN_DEPTH]))
        p = jnp.exp2(s - jnp.max(s, axis=-1, keepdims=True))
        o = _dot(p.astype(BF16), v_ref[h])
        o_ref[h, rows, :] = (o * (1.0 / jnp.sum(p, axis=-1, keepdims=True))).astype(BF16)


def _attn_stage(q_all, k_all, v, batch, s_len):
    t_tok = v.shape[1]
    n_grp = MLA_HEADS // ATTN_HEADS
    return pl.pallas_call(
        _attn_kernel,
        grid=(batch, n_grp),
        in_specs=[pl.BlockSpec((ATTN_HEADS, s_len, QK_PAD), lambda b, g: (g, b, 0)),
                  pl.BlockSpec((ATTN_HEADS, s_len, QK_PAD), lambda b, g: (g, b, 0)),
                  pl.BlockSpec((ATTN_HEADS, s_len, MLA_V), lambda b, g: (g, b, 0))],
        out_specs=pl.BlockSpec((ATTN_HEADS, s_len, MLA_V), lambda b, g: (g, b, 0)),
        out_shape=jax.ShapeDtypeStruct((MLA_HEADS, t_tok, MLA_V), BF16),
        compiler_params=pltpu.CompilerParams(dimension_semantics=("arbitrary",) * 2,
                                             vmem_limit_bytes=VMEM_LIMIT),
        name="attn_stage",
    )(q_all, k_all, v)


def _output_kernel(x_ref, a_ref, ob_ref, g_mla_ref, w_oa_ref, w_ob_ref, g_ffn_ref, w_up_ref,
                   w_dn_ref, out_ref):
    ob = _rms(_load_slabs(ob_ref).astype(F32), g_mla_ref[...]).astype(BF16)
    x1 = x_ref[...] + _dot(_load_slabs(a_ref), w_oa_ref[...]) + _dot(ob, w_ob_ref[...])
    h = _rms(x1, g_ffn_ref[...]).astype(BF16)
    up = jnp.maximum(_dot(h, w_up_ref[...]), 0.0)
    out_ref[...] = x1 + _dot((up * up).astype(BF16), w_dn_ref[...])


def _output_stage(x2, a, ob, g_mla, w_oa, w_ob, g_ffn, w_up, w_dn, tm):
    t_tok, d_model = x2.shape
    consts = (g_mla, w_oa, w_ob, g_ffn, w_up, w_dn)
    return pl.pallas_call(
        _output_kernel,
        grid=(t_tok // tm,),
        in_specs=[pl.BlockSpec((tm, d_model), lambda i: (i, 0)),
                  pl.BlockSpec((HG_HEADS, tm, HG_D), lambda i: (0, i, 0)),
                  pl.BlockSpec((MLA_HEADS, tm, MLA_V), lambda i: (0, i, 0))]
                 + [_const_spec(c.shape) for c in consts],
        out_specs=pl.BlockSpec((tm, d_model), lambda i: (i, 0)),
        out_shape=jax.ShapeDtypeStruct((t_tok, d_model), F32),
        compiler_params=pltpu.CompilerParams(dimension_semantics=("arbitrary",),
                                             vmem_limit_bytes=VMEM_LIMIT),
        name="output_stage",
    )(x2, a, ob, *consts)


def kernel(x, positions, g_mix_norm, w_in, lb_param, g_hgrn_out, g_cq, w_q_up, g_ckv, w_kv_up,
           g_q_norm, g_k_norm, g_mla_out, w_out, g_ffn_norm, w_up, w_down):
    batch, s_len, d_model = x.shape
    assert g_mix_norm.shape[0] == 1 and lb_param.shape[1] == 2, "one layer"
    assert s_len % (HG_CHUNK * HG_UNROLL) == 0
    t_tok = batch * s_len
    row = lambda t: t.reshape(1, -1).astype(F32)

    w = w_in[0]
    o_q, o_ff, o_fb, o_i, o_g, o_cq = (HG_W * n for n in range(6))
    o_ckv = o_cq + Q_LORA
    o_kr = o_ckv + KV_LORA
    w_qig = jnp.concatenate([w[:, o_q:o_ff], w[:, o_i:o_g], w[:, o_g:o_cq]], axis=1).astype(BF16)
    w_f = w[:, o_ff:o_i].astype(BF16)
    half = MLA_ROPE // 2
    swap = lambda t: jnp.concatenate([t[..., half:], t[..., :half]], axis=-1)
    pad_lanes = lambda t: jnp.pad(t, [(0, 0)] * (t.ndim - 1) + [(0, LANES - MLA_ROPE)])
    w_kr = w[:, o_kr:]
    w_mla = jnp.concatenate([w[:, o_cq:o_kr], pad_lanes(w_kr), pad_lanes(swap(w_kr))],
                            axis=1).astype(BF16)

    wq = w_q_up[0].reshape(Q_LORA, MLA_HEADS, MLA_QK)
    wq_rope = wq[:, :, MLA_NOPE:]
    w_qup = jnp.concatenate([wq[:, :, :MLA_NOPE].reshape(Q_LORA, -1),
                             pad_lanes(wq_rope).reshape(Q_LORA, -1),
                             pad_lanes(swap(wq_rope)).reshape(Q_LORA, -1)], axis=1).astype(BF16)
    wkv = w_kv_up[0].reshape(KV_LORA, MLA_HEADS, MLA_NOPE + MLA_V)
    w_kvup = jnp.concatenate([wkv[:, :, :MLA_NOPE].reshape(KV_LORA, -1),
                              wkv[:, :, MLA_NOPE:].reshape(KV_LORA, -1)], axis=1).astype(BF16)
    rope_gain = lambda t: jnp.stack([pad_lanes(t), pad_lanes(swap(t))]).astype(F32)
    g_qn, g_qr = row(g_q_norm[0, :MLA_NOPE]), rope_gain(g_q_norm[0, MLA_NOPE:])
    g_kn, g_kr = row(g_k_norm[0, :MLA_NOPE]), rope_gain(g_k_norm[0, MLA_NOPE:])
    invf = (ROPE_THETA ** (-jnp.arange(0, MLA_ROPE, 2, dtype=F32) / MLA_ROPE)).reshape(-1, 1)

    x2 = x.reshape(t_tok, d_model)
    pos = positions.reshape(1, t_tok).astype(F32)
    tm = min(512, s_len)
    qig, fgate, q_all, k_all, v = _input_stage(
        x2, pos, invf, row(g_mix_norm[0]), w_qig, w_f, w_mla, row(g_cq[0]), w_qup, row(g_ckv[0]),
        w_kvup, g_qn, g_qr, g_kn, g_kr, tm)

    a = _hgrn_stage(qig, fgate, lb_param.astype(F32), g_hgrn_out[0].reshape(1, HG_W).astype(F32),
                    batch, s_len)
    ob = _attn_stage(q_all, k_all, v, batch, s_len)

    wo = w_out[0].astype(BF16)
    out = _output_stage(x2, a, ob, row(g_mla_out[0]), wo[:HG_W], wo[HG_W:], row(g_ffn_norm[0]),
                        w_up[0].astype(BF16), w_down[0].astype(BF16), tm)
    return out.reshape(batch, s_len, d_model)
```

```python
import functools

import jax
import jax.numpy as jnp
from jax import lax
from jax.experimental import pallas as pl
from jax.experimental.pallas import tpu as pltpu

NORM_EPS = 1e-6
HG_HEADS = 4
HG_D = 128
HG_W = HG_HEADS * HG_D
MLA_HEADS = 4
MLA_NOPE = 128
MLA_ROPE = 64
MLA_V = 128
MLA_QK = MLA_NOPE + MLA_ROPE
Q_LORA = 384
KV_LORA = 256
ROPE_THETA = 10000.0
LANES = 128
QK_PAD = 2 * LANES

HG_CHUNK = 128
HG_LEAF = 8
HG_BLOCK = 32
HG_MAX_BLOCK_DECAY = 100.0
HG_UNROLL = 8
ATTN_SUB = 512
ATTN_HEADS = 2
ATTN_DEPTH = 1
INPUT_TILE = 512
INPUT_SUB = 256
LOG2_E = 1.4426950408889634
VMEM_LIMIT = 56 * 1024 * 1024

F32 = jnp.float32
BF16 = jnp.bfloat16


def _dot(a, b):
    return jnp.dot(a, b, preferred_element_type=F32)


def _dot_nt(a, b):
    return lax.dot_general(a, b, (((1,), (1,)), ((), ())), preferred_element_type=F32)


def _rms(t, gain):
    return t * lax.rsqrt(jnp.mean(t * t, axis=-1, keepdims=True) + NORM_EPS) * gain


def _sigmoid(t):
    return 1.0 / (1.0 + jnp.exp2(t * -LOG2_E))


def _store_slabs(ref, rows, t):
    for j in range(ref.shape[0]):
        ref[j, rows, :] = t[:, j * LANES:(j + 1) * LANES].astype(ref.dtype)


def _load_slabs(ref):
    return jnp.concatenate([ref[j] for j in range(ref.shape[0])], axis=1)


def _input_kernel(x_ref, pos_ref, invf_ref, g_mix_ref, w_qig_ref, w_f_ref, w_mla_ref, g_cq_ref,
                  w_qup_ref, g_ckv_ref, w_kvup_ref, g_qn_ref, g_qr_ref, g_kn_ref, g_kr_ref,
                  qig_ref, f_ref, qa_ref, ka_ref, v_ref):
    tm = x_ref.shape[0]
    sub = min(INPUT_SUB, tm)
    tiles = [dict(rows=slice(i * sub, (i + 1) * sub)) for i in range(tm // sub)]
    scale = MLA_QK ** -0.5 * LOG2_E

    def project(t):
        h = _rms(x_ref[t["rows"], :], g_mix_ref[...]).astype(BF16)
        _store_slabs(qig_ref, t["rows"], _dot(h, w_qig_ref[...]))
        _store_slabs(f_ref, t["rows"], _dot(h, w_f_ref[...]))
        t["m"] = _dot(h, w_mla_ref[...])

    def up_project(t):
        c_q = t["m"][:, :Q_LORA]
        c_kv = t["m"][:, Q_LORA:Q_LORA + KV_LORA]
        t["q"] = _dot(_rms(c_q, g_cq_ref[...]).astype(BF16), w_qup_ref[...])
        t["kv"] = _dot(_rms(c_kv, g_ckv_ref[...]).astype(BF16), w_kvup_ref[...])

    def rope_tables():
        ang = invf_ref[...] * pos_ref[...]
        c, s = jnp.cos(ang), jnp.sin(ang)
        z = jnp.zeros_like(c)
        return jnp.concatenate([c, c, z, z], axis=0).T, jnp.concatenate([-s, s, z, z], axis=0).T

    def heads(t, cos_t, sin_t):
        rows, q, kv = t["rows"], t["q"], t["kv"]
        _store_slabs(v_ref, rows, kv[:, MLA_HEADS * MLA_NOPE:])
        cos_r, sin_r = cos_t[rows, :], sin_t[rows, :]
        q_cos, q_sin = g_qr_ref[0:1] * cos_r, g_qr_ref[1:2] * sin_r
        rope_cols = Q_LORA + KV_LORA
        k_r = t["m"][:, rope_cols:rope_cols + LANES]
        k_pe = (k_r * (g_kr_ref[0:1] * cos_r)
                + t["m"][:, rope_cols + LANES:] * (g_kr_ref[1:2] * sin_r))
        sq_kr = k_r * k_r
        for hd in range(MLA_HEADS):
            lo, hi = hd * LANES, (hd + 1) * LANES
            q_n = q[:, lo:hi]
            q_r = q[:, MLA_HEADS * MLA_NOPE + lo:MLA_HEADS * MLA_NOPE + hi]
            q_r_swapped = q[:, 2 * MLA_HEADS * MLA_NOPE + lo:2 * MLA_HEADS * MLA_NOPE + hi]
            ssq = jnp.sum(q_n * q_n + q_r * q_r, axis=-1, keepdims=True)
            rstd = lax.rsqrt(ssq * (1.0 / MLA_QK) + NORM_EPS) * scale
            q_pe = q_r * q_cos + q_r_swapped * q_sin
            qa_ref[hd, rows, :] = jnp.concatenate([q_n * g_qn_ref[...] * rstd, q_pe * rstd],
                                                  axis=1).astype(BF16)
            k_n = kv[:, lo:hi]
            ssq = jnp.sum(k_n * k_n + sq_kr, axis=-1, keepdims=True)
            rstd = lax.rsqrt(ssq * (1.0 / MLA_QK) + NORM_EPS)
            ka_ref[hd, rows, :] = jnp.concatenate([k_n * g_kn_ref[...] * rstd, k_pe * rstd],
                                                  axis=1).astype(BF16)

    tables = None
    for i, t in enumerate(tiles):
        project(t)
        if i == 0:
            tables = rope_tables()
        else:
            heads(tiles[i - 1], *tables)
        up_project(t)
    heads(tiles[-1], *tables)


def _const_spec(shape):
    nd = len(shape)
    return pl.BlockSpec(shape, lambda *_: (0,) * nd, pipeline_mode=pl.Buffered(1))


def _input_stage(x2, pos, invf, g_mix, w_qig, w_f, w_mla, g_cq, w_qup, g_ckv, w_kvup,
                 g_qn, g_qr, g_kn, g_kr, tm):
    t_tok, d_model = x2.shape
    consts = (invf, g_mix, w_qig, w_f, w_mla, g_cq, w_qup, g_ckv, w_kvup, g_qn, g_qr, g_kn, g_kr)
    return pl.pallas_call(
        _input_kernel,
        grid=(t_tok // tm,),
        in_specs=[pl.BlockSpec((tm, d_model), lambda i: (i, 0)),
                  pl.BlockSpec((1, tm), lambda i: (0, i))] + [_const_spec(c.shape) for c in consts],
        out_specs=[pl.BlockSpec((3 * HG_HEADS, tm, HG_D), lambda i: (0, i, 0)),
                   pl.BlockSpec((2 * HG_HEADS, tm, HG_D), lambda i: (0, i, 0)),
                   pl.BlockSpec((MLA_HEADS, tm, QK_PAD), lambda i: (0, i, 0)),
                   pl.BlockSpec((MLA_HEADS, tm, QK_PAD), lambda i: (0, i, 0)),
                   pl.BlockSpec((MLA_HEADS, tm, MLA_V), lambda i: (0, i, 0))],
        out_shape=[jax.ShapeDtypeStruct((3 * HG_HEADS, t_tok, HG_D), BF16),
                   jax.ShapeDtypeStruct((2 * HG_HEADS, t_tok, HG_D), F32),
                   jax.ShapeDtypeStruct((MLA_HEADS, t_tok, QK_PAD), BF16),
                   jax.ShapeDtypeStruct((MLA_HEADS, t_tok, QK_PAD), BF16),
                   jax.ShapeDtypeStruct((MLA_HEADS, t_tok, MLA_V), BF16)],
        compiler_params=pltpu.CompilerParams(dimension_semantics=("arbitrary",),
                                             vmem_limit_bytes=VMEM_LIMIT),
        name="input_stage",
    )(x2, pos, *consts)


def _hg_levels(leaf):
    sizes = []
    half = HG_CHUNK // 2
    while half >= leaf:
        sizes.append(half)
        half //= 2
    return tuple(sizes)


def _hg_gates(logit, lb, tri):
    f = lb + (1.0 - lb) * _sigmoid(logit)
    k = 1.0 - f
    g = jnp.log2(f)
    g_hi = g.astype(BF16)
    r1 = g - g_hi.astype(F32)
    g_mid = r1.astype(BF16)
    g_lo = (r1 - g_mid.astype(F32)).astype(BF16)
    return k, _dot(tri, jnp.concatenate([g_hi, g_mid, g_lo], axis=0))


def _hg_block_decay(b, rev):
    n_blk = HG_CHUNK // HG_BLOCK
    edge = [b[i * HG_BLOCK:i * HG_BLOCK + 1] if rev else b[(i + 1) * HG_BLOCK - 1:(i + 1) * HG_BLOCK]
            for i in range(n_blk)]
    worst = edge[-1] if rev else edge[0]
    for i in range(1, n_blk):
        worst = jnp.minimum(worst, edge[i - 1] - edge[i] if rev else edge[i] - edge[i - 1])
    return worst


def _hg_chunks(jobs, lvl_masks, blocked):
    c = HG_CHUNK
    for j in jobs:
        q, k, b = j["q"], j["k"], j["b"]
        b_tot = b[0:1] if j["rev"] else b[c - 1:c]
        j["dec"] = jnp.exp2(b_tot)
        j["q_in"] = (q * jnp.exp2(b)).astype(BF16)
        k_out = (k * jnp.exp2(b_tot - b)).astype(BF16)
        j["st_add"] = _dot(j["v"].T.astype(BF16), k_out)
        j["p"] = None

    for lvl, half in enumerate(_hg_levels(HG_BLOCK if blocked else HG_LEAF)):
        n_grp = c // (2 * half)
        shape3 = (n_grp, 2 * half, HG_D)
        zero = jnp.zeros((n_grp, half, HG_D), F32)
        for j in jobs:
            b3, q3, k3 = j["b"].reshape(shape3), j["q"].reshape(shape3), j["k"].reshape(shape3)
            if j["rev"]:
                mid = b3[:, half:half + 1]
                q_l = jnp.concatenate([q3[:, :half] * jnp.exp2(b3[:, :half] - mid), zero], axis=1)
                k_l = jnp.concatenate([zero, k3[:, half:] * jnp.exp2(mid - b3[:, half:])], axis=1)
            else:
                mid = b3[:, half - 1:half]
                q_l = jnp.concatenate([zero, q3[:, half:] * jnp.exp2(b3[:, half:] - mid)], axis=1)
                k_l = jnp.concatenate([k3[:, :half] * jnp.exp2(mid - b3[:, :half]), zero], axis=1)
            p_l = _dot_nt(q_l.reshape(c, HG_D).astype(BF16), k_l.reshape(c, HG_D).astype(BF16))
            p_l = p_l * lvl_masks[lvl] if n_grp > 1 else p_l
            j["p"] = p_l if j["p"] is None else j["p"] + p_l

    if blocked:
        n_blk = c // HG_BLOCK
        shape3 = (n_blk, HG_BLOCK, HG_D)
        zero_row = jnp.zeros((1, 1, HG_D), F32)
        for j in jobs:
            b3 = j["b"].reshape(shape3)
            if j["rev"]:
                ref = jnp.concatenate([b3[1:, 0:1], zero_row], axis=0)
            else:
                ref = jnp.concatenate([zero_row, b3[:-1, HG_BLOCK - 1:HG_BLOCK]], axis=0)
            b_loc = b3 - ref
            q_b = (j["q"].reshape(shape3) * jnp.exp2(b_loc)).reshape(c, HG_D).astype(BF16)
            k_b = (j["k"].reshape(shape3) * jnp.exp2(-b_loc)).reshape(c, HG_D).astype(BF16)
            j["p_leaf"] = _dot_nt(q_b, k_b) * j["block_mask"]
    else:
        for j in jobs:
            j["kpad"][HG_LEAF:HG_LEAF + c, :] = j["k"]
            j["bpad"][HG_LEAF:HG_LEAF + c, :] = j["b"]
        for j in jobs:
            w = [(j["q"] * j["k"]).astype(BF16)]
            for d in range(1, HG_LEAF):
                lo = HG_LEAF + d if j["rev"] else HG_LEAF - d
                k_s = j["kpad"][lo:lo + c, :]
                b_s = j["bpad"][lo:lo + c, :]
                w.append((j["q"] * k_s * jnp.exp2(j["b"] - b_s)).astype(BF16))
            j["row_sums"] = _dot(jnp.concatenate(w, axis=1), j["spread"])
        for j in jobs:
            j["p_leaf"] = (pltpu.roll(j["row_sums"], 0, 1, stride=1, stride_axis=0)
                           * j["leaf_mask"])

    outs = []
    for j in jobs:
        p = (j["p"] + j["p_leaf"]).astype(BF16)
        outs.append((_dot(p, j["v"].astype(BF16)), j["q_in"], j["st_add"], j["dec"]))
    return outs


def _hgrn_kernel(q_ref, i_ref, g_ref, ff_ref, fb_ref, lbp_ref, gain_ref, tri_ref, lvl_ref, leaf_ref,
                 block_ref, spread_ref, out_ref, k_ref, b_ref, o_ref, qin_ref, add_ref, dec_ref,
                 st_ref, kpad_ref, bpad_ref):
    s_len = q_ref.shape[0]
    n_chunk = s_len // HG_CHUNK
    logit_refs = (ff_ref, fb_ref)
    chunk_rows = lambda ci: pl.ds(pl.multiple_of(ci * HG_CHUNK, HG_CHUNK), HG_CHUNK)

    def lower_bound(d):
        p = lbp_ref[d]
        e = jnp.exp(p - jnp.max(p, axis=0, keepdims=True))
        return e[0:1] / jnp.sum(e, axis=0, keepdims=True)

    lbs = (lower_bound(0), lower_bound(1))

    def gates(n, worst):
        for u in range(HG_UNROLL):
            rows = chunk_rows(n * HG_UNROLL + u)
            for d in range(2):
                k, b = _hg_gates(logit_refs[d][rows, :], lbs[d], tri_ref[d])
                k_ref[d, rows, :] = k
                b_ref[d, rows, :] = b
                worst = jnp.minimum(worst, _hg_block_decay(b, bool(d)))
        return worst

    worst = lax.fori_loop(0, n_chunk // HG_UNROLL, gates, jnp.zeros((1, HG_D), F32))
    blocks_representable = jnp.min(worst) >= -HG_MAX_BLOCK_DECAY

    def within_chunks(blocked):
        def body(n, carry):
            lvl_masks = [lvl_ref[l] for l in range(lvl_ref.shape[0])]
            jobs = []
            for u in range(HG_UNROLL):
                ci = n * HG_UNROLL + u
                rows = chunk_rows(ci)
                for d, rev in enumerate((False, True)):
                    slot = d * HG_UNROLL + u
                    jobs.append(dict(
                        ci=ci, rows=rows, d=d, rev=rev, q=q_ref[rows, :].astype(F32),
                        k=k_ref[d, rows, :], b=b_ref[d, rows, :], v=i_ref[rows, :].astype(F32),
                        kpad=kpad_ref.at[slot], bpad=bpad_ref.at[slot], leaf_mask=leaf_ref[d],
                        block_mask=block_ref[d], spread=spread_ref[d]))
            for j, (o, q_in, st_add, dec) in zip(jobs, _hg_chunks(jobs, lvl_masks, blocked)):
                o_ref[j["d"], j["rows"], :] = o
                qin_ref[j["rows"], j["d"] * HG_D:(j["d"] + 1) * HG_D] = q_in
                add_ref[j["d"], j["ci"]] = st_add
                dec_ref[j["d"], j["ci"]] = dec
            return carry
        return body

    @pl.when(blocks_representable)
    def _():
        lax.fori_loop(0, n_chunk // HG_UNROLL, within_chunks(True), 0)

    @pl.when(jnp.logical_not(blocks_representable))
    def _():
        kpad_ref[...] = jnp.zeros_like(kpad_ref)
        bpad_ref[...] = jnp.zeros_like(bpad_ref)
        lax.fori_loop(0, n_chunk // HG_UNROLL, within_chunks(False), 0)

    def across_chunks(n, states):
        new_states = []
        for d, st in enumerate(states):
            ci = (n_chunk - 1 - n) if d else n
            st_ref[ci, :, d * HG_D:(d + 1) * HG_D] = st.astype(BF16)
            new_states.append(dec_ref[d, ci] * st + add_ref[d, ci])
        return tuple(new_states)

    zero_state = jnp.zeros((HG_D, HG_D), F32)
    lax.fori_loop(0, n_chunk, across_chunks, (zero_state, zero_state), unroll=True)

    def finish(n, carry):
        rows = chunk_rows(n)
        o = o_ref[0, rows, :] + o_ref[1, rows, :] + _dot_nt(qin_ref[rows, :], st_ref[n])
        y = _rms(o, gain_ref[...])
        gate = g_ref[rows, :].astype(F32)
        out_ref[rows, :] = (y * gate * _sigmoid(gate)).astype(BF16)
        return carry

    lax.fori_loop(0, n_chunk, finish, 0, unroll=8)


def _hgrn_stage(qig, fgate, lb_param, gain, batch, s_len):
    assert HG_CHUNK == LANES, "the leaf path rotates each row of a (C, C) tile by its row index"
    t_tok = qig.shape[1]
    c = HG_CHUNK
    r = jnp.arange(c)
    t_i, s_i = r[:, None], r[None, :]
    lower = s_i <= t_i
    causal = jnp.stack([lower, lower.T])
    tri = jnp.tile(causal, (1, 1, 3)).astype(BF16)
    lvl_masks = jnp.stack([t_i // (2 * h) == s_i // (2 * h) for h in _hg_levels(HG_LEAF)]).astype(F32)
    leaf_mask = (causal & (t_i // HG_LEAF == s_i // HG_LEAF)).astype(F32)
    block_mask = (causal & (t_i // HG_BLOCK == s_i // HG_BLOCK)).astype(F32)
    dist = jnp.repeat(jnp.arange(HG_LEAF), HG_D)[:, None]
    spread = jnp.stack([(-s_i) % HG_LEAF == dist, s_i % HG_LEAF == dist]).astype(BF16)
    blk = lambda col: pl.BlockSpec((None, s_len, HG_D),
                                   lambda b, h, col=col: (col * HG_HEADS + h, b, 0))
    consts = (tri, lvl_masks, leaf_mask, block_mask, spread)
    seq = lambda dtype: pltpu.VMEM((2, s_len, HG_D), dtype)
    return pl.pallas_call(
        _hgrn_kernel,
        grid=(batch, HG_HEADS),
        in_specs=[blk(0), blk(1), blk(2), blk(0), blk(1),
                  pl.BlockSpec((2, 2, HG_D), lambda b, h: (0, 0, h)),
                  pl.BlockSpec((1, HG_D), lambda b, h: (0, h))] + [_const_spec(t.shape) for t in consts],
        out_specs=pl.BlockSpec((None, s_len, HG_D), lambda b, h: (h, b, 0)),
        out_shape=jax.ShapeDtypeStruct((HG_HEADS, t_tok, HG_D), BF16),
        scratch_shapes=[seq(F32), seq(F32), seq(F32), pltpu.VMEM((s_len, 2 * HG_D), BF16),
                        pltpu.VMEM((2, s_len // c, HG_D, HG_D), F32),
                        pltpu.VMEM((2, s_len // c, 1, HG_D), F32),
                        pltpu.VMEM((s_len // c, HG_D, 2 * HG_D), BF16),
                        pltpu.VMEM((2 * HG_UNROLL, c + 2 * HG_LEAF, HG_D), F32),
                        pltpu.VMEM((2 * HG_UNROLL, c + 2 * HG_LEAF, HG_D), F32)],
        compiler_params=pltpu.CompilerParams(dimension_semantics=("arbitrary", "arbitrary"),
                                             vmem_limit_bytes=VMEM_LIMIT),
        name="hgrn_stage",
    )(qig, qig, qig, fgate, fgate, lb_param, gain, *consts)


def _attn_kernel(q_ref, k_ref, v_ref, o_ref):
    n_heads, s_len, _ = q_ref.shape
    sub = min(ATTN_SUB, s_len)
    items = [(h, slice(i * sub, (i + 1) * sub)) for h in range(n_heads) for i in range(s_len // sub)]
    scores = lambda h, rows: _dot_nt(q_ref[h, rows, :], k_ref[h])
    pending = [scores(*item) for item in items[:ATTN_DEPTH]]
    for n, (h, rows) in enumerate(items):
        s = pending.pop(0)
        if n + ATTN_DEPTH < len(items):
            pending.append(scores(*items[n + ATTN_DEPTH]))
        p = jnp.exp2(s - jnp.max(s, axis=-1, keepdims=True))
        o = _dot(p.astype(BF16), v_ref[h])
        o_ref[h, rows, :] = (o * (1.0 / jnp.sum(p, axis=-1, keepdims=True))).astype(BF16)


def _attn_stage(q_all, k_all, v, batch, s_len):
    t_tok = v.shape[1]
    n_grp = MLA_HEADS // ATTN_HEADS
    return pl.pallas_call(
        _attn_kernel,
        grid=(batch, n_grp),
        in_specs=[pl.BlockSpec((ATTN_HEADS, s_len, QK_PAD), lambda b, g: (g, b, 0)),
                  pl.BlockSpec((ATTN_HEADS, s_len, QK_PAD), lambda b, g: (g, b, 0)),
                  pl.BlockSpec((ATTN_HEADS, s_len, MLA_V), lambda b, g: (g, b, 0))],
        out_specs=pl.BlockSpec((ATTN_HEADS, s_len, MLA_V), lambda b, g: (g, b, 0)),
        out_shape=jax.ShapeDtypeStruct((MLA_HEADS, t_tok, MLA_V), BF16),
        compiler_params=pltpu.CompilerParams(dimension_semantics=("arbitrary",) * 2,
                                             vmem_limit_bytes=VMEM_LIMIT),
        name="attn_stage",
    )(q_all, k_all, v)


def _output_kernel(x_ref, a_ref, ob_ref, g_mla_ref, w_oa_ref, w_ob_ref, g_ffn_ref, w_up_ref,
                   w_dn_ref, out_ref):
    ob = _rms(_load_slabs(ob_ref).astype(F32), g_mla_ref[...]).astype(BF16)
    x1 = x_ref[...] + _dot(_load_slabs(a_ref), w_oa_ref[...]) + _dot(ob, w_ob_ref[...])
    h = _rms(x1, g_ffn_ref[...]).astype(BF16)
    up = jnp.maximum(_dot(h, w_up_ref[...]), 0.0)
    out_ref[...] = x1 + _dot((up * up).astype(BF16), w_dn_ref[...])


def _output_stage(x2, a, ob, g_mla, w_oa, w_ob, g_ffn, w_up, w_dn, tm):
    t_tok, d_model = x2.shape
    consts = (g_mla, w_oa, w_ob, g_ffn, w_up, w_dn)
    return pl.pallas_call(
        _output_kernel,
        grid=(t_tok // tm,),
        in_specs=[pl.BlockSpec((tm, d_model), lambda i: (i, 0)),
                  pl.BlockSpec((HG_HEADS, tm, HG_D), lambda i: (0, i, 0)),
                  pl.BlockSpec((MLA_HEADS, tm, MLA_V), lambda i: (0, i, 0))]
                 + [_const_spec(c.shape) for c in consts],
        out_specs=pl.BlockSpec((tm, d_model), lambda i: (i, 0)),
        out_shape=jax.ShapeDtypeStruct((t_tok, d_model), F32),
        compiler_params=pltpu.CompilerParams(dimension_semantics=("arbitrary",),
                                             vmem_limit_bytes=VMEM_LIMIT),
        name="output_stage",
    )(x2, a, ob, *consts)


def kernel(x, positions, g_mix_norm, w_in, lb_param, g_hgrn_out, g_cq, w_q_up, g_ckv, w_kv_up,
           g_q_norm, g_k_norm, g_mla_out, w_out, g_ffn_norm, w_up, w_down):
    batch, s_len, d_model = x.shape
    assert g_mix_norm.shape[0] == 1 and lb_param.shape[1] == 2, "one layer"
    assert s_len % (HG_CHUNK * HG_UNROLL) == 0
    t_tok = batch * s_len
    row = lambda t: t.reshape(1, -1).astype(F32)

    w = w_in[0]
    o_q, o_ff, o_fb, o_i, o_g, o_cq = (HG_W * n for n in range(6))
    o_ckv = o_cq + Q_LORA
    o_kr = o_ckv + KV_LORA
    w_qig = jnp.concatenate([w[:, o_q:o_ff], w[:, o_i:o_g], w[:, o_g:o_cq]], axis=1).astype(BF16)
    w_f = w[:, o_ff:o_i].astype(BF16)
    half = MLA_ROPE // 2
    swap = lambda t: jnp.concatenate([t[..., half:], t[..., :half]], axis=-1)
    pad_lanes = lambda t: jnp.pad(t, [(0, 0)] * (t.ndim - 1) + [(0, LANES - MLA_ROPE)])
    w_kr = w[:, o_kr:]
    w_mla = jnp.concatenate([w[:, o_cq:o_kr], pad_lanes(w_kr), pad_lanes(swap(w_kr))],
                            axis=1).astype(BF16)

    wq = w_q_up[0].reshape(Q_LORA, MLA_HEADS, MLA_QK)
    wq_rope = wq[:, :, MLA_NOPE:]
    w_qup = jnp.concatenate([wq[:, :, :MLA_NOPE].reshape(Q_LORA, -1),
                             pad_lanes(wq_rope).reshape(Q_LORA, -1),
                             pad_lanes(swap(wq_rope)).reshape(Q_LORA, -1)], axis=1).astype(BF16)
    wkv = w_kv_up[0].reshape(KV_LORA, MLA_HEADS, MLA_NOPE + MLA_V)
    w_kvup = jnp.concatenate([wkv[:, :, :MLA_NOPE].reshape(KV_LORA, -1),
                              wkv[:, :, MLA_NOPE:].reshape(KV_LORA, -1)], axis=1).astype(BF16)
    rope_gain = lambda t: jnp.stack([pad_lanes(t), pad_lanes(swap(t))]).astype(F32)
    g_qn, g_qr = row(g_q_norm[0, :MLA_NOPE]), rope_gain(g_q_norm[0, MLA_NOPE:])
    g_kn, g_kr = row(g_k_norm[0, :MLA_NOPE]), rope_gain(g_k_norm[0, MLA_NOPE:])
    invf = (ROPE_THETA ** (-jnp.arange(0, MLA_ROPE, 2, dtype=F32) / MLA_ROPE)).reshape(-1, 1)

    x2 = x.reshape(t_tok, d_model)
    pos = positions.reshape(1, t_tok).astype(F32)
    tm = min(512, s_len)
    qig, fgate, q_all, k_all, v = _input_stage(
        x2, pos, invf, row(g_mix_norm[0]), w_qig, w_f, w_mla, row(g_cq[0]), w_qup, row(g_ckv[0]),
        w_kvup, g_qn, g_qr, g_kn, g_kr, min(INPUT_TILE, s_len))

    a = _hgrn_stage(qig, fgate, lb_param.astype(F32), g_hgrn_out[0].reshape(1, HG_W).astype(F32),
                    batch, s_len)
    ob = _attn_stage(q_all, k_all, v, batch, s_len)

    wo = w_out[0].astype(BF16)
    out = _output_stage(x2, a, ob, row(g_mla_out[0]), wo[:HG_W], wo[HG_W:], row(g_ffn_norm[0]),
                        w_up[0].astype(BF16), w_down[0].astype(BF16), tm)
    return out.reshape(batch, s_len, d_model)
```

```python
import functools

import jax
import jax.numpy as jnp
from jax import lax
from jax.experimental import pallas as pl
from jax.experimental.pallas import tpu as pltpu

NORM_EPS = 1e-6
HG_HEADS = 4
HG_D = 128
HG_W = HG_HEADS * HG_D
MLA_HEADS = 4
MLA_NOPE = 128
MLA_ROPE = 64
MLA_V = 128
MLA_QK = MLA_NOPE + MLA_ROPE
Q_LORA = 384
KV_LORA = 256
ROPE_THETA = 10000.0
LANES = 128
QK_PAD = 2 * LANES

HG_CHUNK = 128
HG_LEAF = 8
HG_BLOCK = 32
HG_MAX_BLOCK_DECAY = 100.0
HG_UNROLL = 8
ATTN_SUB = 512
ATTN_HEADS = 2
ATTN_DEPTH = 1
INPUT_TILE = 512
INPUT_SUB = 256
LOG2_E = 1.4426950408889634
VMEM_LIMIT = 56 * 1024 * 1024

F32 = jnp.float32
BF16 = jnp.bfloat16


def _dot(a, b):
    return jnp.dot(a, b, preferred_element_type=F32)


def _dot_nt(a, b):
    return lax.dot_general(a, b, (((1,), (1,)), ((), ())), preferred_element_type=F32)


def _rms(t, gain):
    return t * lax.rsqrt(jnp.mean(t * t, axis=-1, keepdims=True) + NORM_EPS) * gain


def _sigmoid(t):
    return 1.0 / (1.0 + jnp.exp2(t * -LOG2_E))


def _store_slabs(ref, rows, t):
    for j in range(ref.shape[0]):
        ref[j, rows, :] = t[:, j * LANES:(j + 1) * LANES].astype(ref.dtype)


def _load_slabs(ref):
    return jnp.concatenate([ref[j] for j in range(ref.shape[0])], axis=1)


def _input_kernel(x_ref, pos_ref, invf_ref, g_mix_ref, w_qig_ref, w_f_ref, w_mla_ref, g_cq_ref,
                  w_qup_ref, g_ckv_ref, w_kvup_ref, g_qn_ref, g_qr_ref, g_kn_ref, g_kr_ref,
                  qig_ref, f_ref, qa_ref, ka_ref, v_ref):
    tm = x_ref.shape[0]
    sub = min(INPUT_SUB, tm)
    tiles = [dict(rows=slice(i * sub, (i + 1) * sub)) for i in range(tm // sub)]
    scale = MLA_QK ** -0.5 * LOG2_E

    def project(t):
        h = _rms(x_ref[t["rows"], :], g_mix_ref[...]).astype(BF16)
        _store_slabs(qig_ref, t["rows"], _dot(h, w_qig_ref[...]))
        _store_slabs(f_ref, t["rows"], _dot(h, w_f_ref[...]))
        t["m"] = _dot(h, w_mla_ref[...])

    def up_project(t):
        c_q = t["m"][:, :Q_LORA]
        c_kv = t["m"][:, Q_LORA:Q_LORA + KV_LORA]
        t["q"] = _dot(_rms(c_q, g_cq_ref[...]).astype(BF16), w_qup_ref[...])
        t["kv"] = _dot(_rms(c_kv, g_ckv_ref[...]).astype(BF16), w_kvup_ref[...])

    def rope_tables():
        ang = invf_ref[...] * pos_ref[...]
        c, s = jnp.cos(ang), jnp.sin(ang)
        z = jnp.zeros_like(c)
        return jnp.concatenate([c, c, z, z], axis=0).T, jnp.concatenate([-s, s, z, z], axis=0).T

    def heads(t, cos_t, sin_t):
        rows, q, kv = t["rows"], t["q"], t["kv"]
        _store_slabs(v_ref, rows, kv[:, MLA_HEADS * MLA_NOPE:])
        cos_r, sin_r = cos_t[rows, :], sin_t[rows, :]
        q_cos, q_sin = g_qr_ref[0:1] * cos_r, g_qr_ref[1:2] * sin_r
        rope_cols = Q_LORA + KV_LORA
        k_r = t["m"][:, rope_cols:rope_cols + LANES]
        k_pe = (k_r * (g_kr_ref[0:1] * cos_r)
                + t["m"][:, rope_cols + LANES:] * (g_kr_ref[1:2] * sin_r))
        sq_kr = k_r * k_r
        for hd in range(MLA_HEADS):
            lo, hi = hd * LANES, (hd + 1) * LANES
            q_n = q[:, lo:hi]
            q_r = q[:, MLA_HEADS * MLA_NOPE + lo:MLA_HEADS * MLA_NOPE + hi]
            q_r_swapped = q[:, 2 * MLA_HEADS * MLA_NOPE + lo:2 * MLA_HEADS * MLA_NOPE + hi]
            ssq = jnp.sum(q_n * q_n + q_r * q_r, axis=-1, keepdims=True)
            rstd = lax.rsqrt(ssq * (1.0 / MLA_QK) + NORM_EPS) * scale
            q_pe = q_r * q_cos + q_r_swapped * q_sin
            qa_ref[hd, rows, :] = jnp.concatenate([q_n * g_qn_ref[...] * rstd, q_pe * rstd],
                                                  axis=1).astype(BF16)
            k_n = kv[:, lo:hi]
            ssq = jnp.sum(k_n * k_n + sq_kr, axis=-1, keepdims=True)
            rstd = lax.rsqrt(ssq * (1.0 / MLA_QK) + NORM_EPS)
            ka_ref[hd, rows, :] = jnp.concatenate([k_n * g_kn_ref[...] * rstd, k_pe * rstd],
                                                  axis=1).astype(BF16)

    tables = None
    for i, t in enumerate(tiles):
        project(t)
        if i == 0:
            tables = rope_tables()
        else:
            heads(tiles[i - 1], *tables)
        up_project(t)
    heads(tiles[-1], *tables)


def _const_spec(shape):
    nd = len(shape)
    return pl.BlockSpec(shape, lambda *_: (0,) * nd, pipeline_mode=pl.Buffered(1))


def _input_stage(x2, pos, invf, g_mix, w_qig, w_f, w_mla, g_cq, w_qup, g_ckv, w_kvup,
                 g_qn, g_qr, g_kn, g_kr, tm):
    t_tok, d_model = x2.shape
    consts = (invf, g_mix, w_qig, w_f, w_mla, g_cq, w_qup, g_ckv, w_kvup, g_qn, g_qr, g_kn, g_kr)
    return pl.pallas_call(
        _input_kernel,
        grid=(t_tok // tm,),
        in_specs=[pl.BlockSpec((tm, d_model), lambda i: (i, 0)),
                  pl.BlockSpec((1, tm), lambda i: (0, i))] + [_const_spec(c.shape) for c in consts],
        out_specs=[pl.BlockSpec((3 * HG_HEADS, tm, HG_D), lambda i: (0, i, 0)),
                   pl.BlockSpec((2 * HG_HEADS, tm, HG_D), lambda i: (0, i, 0)),
                   pl.BlockSpec((MLA_HEADS, tm, QK_PAD), lambda i: (0, i, 0)),
                   pl.BlockSpec((MLA_HEADS, tm, QK_PAD), lambda i: (0, i, 0)),
                   pl.BlockSpec((MLA_HEADS, tm, MLA_V), lambda i: (0, i, 0))],
        out_shape=[jax.ShapeDtypeStruct((3 * HG_HEADS, t_tok, HG_D), BF16),
                   jax.ShapeDtypeStruct((2 * HG_HEADS, t_tok, HG_D), F32),
                   jax.ShapeDtypeStruct((MLA_HEADS, t_tok, QK_PAD), BF16),
                   jax.ShapeDtypeStruct((MLA_HEADS, t_tok, QK_PAD), BF16),
                   jax.ShapeDtypeStruct((MLA_HEADS, t_tok, MLA_V), BF16)],
        compiler_params=pltpu.CompilerParams(dimension_semantics=("arbitrary",),
                                             vmem_limit_bytes=VMEM_LIMIT),
        name="input_stage",
    )(x2, pos, *consts)


def _hg_levels(leaf):
    sizes = []
    half = HG_CHUNK // 2
    while half >= leaf:
        sizes.append(half)
        half //= 2
    return tuple(sizes)


def _hg_gate_terms(logit, lb):
    f = lb + (1.0 - lb) * _sigmoid(logit)
    g = jnp.log2(f)
    g_hi = g.astype(BF16)
    r1 = g - g_hi.astype(F32)
    g_mid = r1.astype(BF16)
    g_lo = (r1 - g_mid.astype(F32)).astype(BF16)
    return 1.0 - f, jnp.concatenate([g_hi, g_mid, g_lo], axis=0)


def _hg_block_decay(b, rev):
    n_blk = HG_CHUNK // HG_BLOCK
    edge = [b[i * HG_BLOCK:i * HG_BLOCK + 1] if rev else b[(i + 1) * HG_BLOCK - 1:(i + 1) * HG_BLOCK]
            for i in range(n_blk)]
    worst = edge[-1] if rev else edge[0]
    for i in range(1, n_blk):
        worst = jnp.minimum(worst, edge[i - 1] - edge[i] if rev else edge[i] - edge[i - 1])
    return worst


def _hg_pairs_blocked(jobs):
    c, n_blk = HG_CHUNK, HG_CHUNK // HG_BLOCK
    shape3 = (n_blk, HG_BLOCK, HG_D)
    zero_row = jnp.zeros((1, 1, HG_D), F32)
    zero_blk = jnp.zeros((HG_BLOCK, HG_D), F32)
    rows_bf16 = lambda blocks: jnp.concatenate(blocks, axis=0).astype(BF16)
    halves = [h // HG_BLOCK for h in _hg_levels(HG_BLOCK)]
    for j in jobs:
        rev = j["rev"]
        b3 = j["b"].reshape(shape3)
        if rev:
            end = b3[:, 0:1]
            ref = jnp.concatenate([end[1:], zero_row], axis=0)
        else:
            end = b3[:, HG_BLOCK - 1:HG_BLOCK]
            ref = jnp.concatenate([zero_row, end[:-1]], axis=0)
        order = [n_blk - 1 - i for i in range(n_blk)] if rev else list(range(n_blk))
        block_at = {o: i for i, o in enumerate(order)}
        b_tot = end[block_at[n_blk - 1]]
        q_b = j["q"].reshape(shape3) * jnp.exp2(b3 - ref)
        k_f = j["k"].reshape(shape3) * jnp.exp2(end - b3)
        j["dec"] = jnp.exp2(b_tot)
        j["q_in"] = rows_bf16([q_b[i] * jnp.exp2(ref[i]) for i in range(n_blk)])
        j["k_out"] = rows_bf16([k_f[i] * jnp.exp2(b_tot - end[i]) for i in range(n_blk)])
        k_blk = rows_bf16([k_f[i] * jnp.exp2(ref[i] - end[i]) for i in range(n_blk)])
        j["p"] = _dot_nt(q_b.reshape(c, HG_D).astype(BF16), k_blk) * j["block_mask"]
        q_x, k_x = [], []
        for half in halves:
            for first in range(0, n_blk, 2 * half):
                mid = end[block_at[first + half - 1]]
                q_l, k_l = [], []
                for i, o in enumerate(order):
                    late = first + half <= o < first + 2 * half
                    early = first <= o < first + half
                    q_l.append(zero_blk if not late else q_b[i] if o == first + half
                               else q_b[i] * jnp.exp2(ref[i] - mid))
                    k_l.append(zero_blk if not early else k_f[i] if o == first + half - 1
                               else k_f[i] * jnp.exp2(mid - end[i]))
                q_x.append(rows_bf16(q_l))
                k_x.append(rows_bf16(k_l))
        j["cross"] = (jnp.concatenate(q_x, axis=1), jnp.concatenate(k_x, axis=1))

    for j in jobs:
        j["p"] = j["p"] + _dot_nt(*j["cross"])

    for fwd, bwd in zip(jobs[0::2], jobs[1::2]):
        st_add = _dot(fwd["v"].T.astype(BF16), jnp.concatenate([fwd["k_out"], bwd["k_out"]], axis=1))
        fwd["st_add"], bwd["st_add"] = st_add[:, :HG_D], st_add[:, HG_D:]
    for fwd, bwd in zip(jobs[0::2], jobs[1::2]):
        p = jnp.concatenate([fwd["p"], bwd["p"]], axis=0).astype(BF16)
        o = _dot(p, fwd["v"].astype(BF16))
        fwd["o"], bwd["o"] = o[:c], o[c:]


def _hg_chunks(jobs, lvl_masks, blocked):
    c = HG_CHUNK
    if blocked:
        _hg_pairs_blocked(jobs)
        return [(j["o"], j["q_in"], j["st_add"], j["dec"]) for j in jobs]

    for j in jobs:
        q, k, b = j["q"], j["k"], j["b"]
        b_tot = b[0:1] if j["rev"] else b[c - 1:c]
        j["dec"] = jnp.exp2(b_tot)
        j["q_in"] = (q * jnp.exp2(b)).astype(BF16)
        k_out = (k * jnp.exp2(b_tot - b)).astype(BF16)
        j["st_add"] = _dot(j["v"].T.astype(BF16), k_out)
        j["p"] = None

    for lvl, half in enumerate(_hg_levels(HG_LEAF)):
        n_grp = c // (2 * half)
        shape3 = (n_grp, 2 * half, HG_D)
        zero = jnp.zeros((n_grp, half, HG_D), F32)
        for j in jobs:
            b3, q3, k3 = j["b"].reshape(shape3), j["q"].reshape(shape3), j["k"].reshape(shape3)
            if j["rev"]:
                mid = b3[:, half:half + 1]
                q_l = jnp.concatenate([q3[:, :half] * jnp.exp2(b3[:, :half] - mid), zero], axis=1)
                k_l = jnp.concatenate([zero, k3[:, half:] * jnp.exp2(mid - b3[:, half:])], axis=1)
            else:
                mid = b3[:, half - 1:half]
                q_l = jnp.concatenate([zero, q3[:, half:] * jnp.exp2(b3[:, half:] - mid)], axis=1)
                k_l = jnp.concatenate([k3[:, :half] * jnp.exp2(mid - b3[:, :half]), zero], axis=1)
            p_l = _dot_nt(q_l.reshape(c, HG_D).astype(BF16), k_l.reshape(c, HG_D).astype(BF16))
            p_l = p_l * lvl_masks[lvl] if n_grp > 1 else p_l
            j["p"] = p_l if j["p"] is None else j["p"] + p_l

    for j in jobs:
        j["kpad"][HG_LEAF:HG_LEAF + c, :] = j["k"]
        j["bpad"][HG_LEAF:HG_LEAF + c, :] = j["b"]
    for j in jobs:
        w = [(j["q"] * j["k"]).astype(BF16)]
        for d in range(1, HG_LEAF):
            lo = HG_LEAF + d if j["rev"] else HG_LEAF - d
            k_s = j["kpad"][lo:lo + c, :]
            b_s = j["bpad"][lo:lo + c, :]
            w.append((j["q"] * k_s * jnp.exp2(j["b"] - b_s)).astype(BF16))
        j["row_sums"] = _dot(jnp.concatenate(w, axis=1), j["spread"])
    for j in jobs:
        j["p_leaf"] = (pltpu.roll(j["row_sums"], 0, 1, stride=1, stride_axis=0)
                       * j["leaf_mask"])

    outs = []
    for j in jobs:
        p = (j["p"] + j["p_leaf"]).astype(BF16)
        outs.append((_dot(p, j["v"].astype(BF16)), j["q_in"], j["st_add"], j["dec"]))
    return outs


def _hgrn_kernel(q_ref, i_ref, g_ref, ff_ref, fb_ref, lbp_ref, gain_ref, tri_ref, lvl_ref, leaf_ref,
                 block_ref, spread_ref, out_ref, k_ref, b_ref, o_ref, qin_ref, add_ref, dec_ref,
                 st_ref, kpad_ref, bpad_ref):
    s_len = q_ref.shape[0]
    n_chunk = s_len // HG_CHUNK
    logit_refs = (ff_ref, fb_ref)
    chunk_rows = lambda ci: pl.ds(pl.multiple_of(ci * HG_CHUNK, HG_CHUNK), HG_CHUNK)

    def lower_bound(d):
        p = lbp_ref[d]
        e = jnp.exp(p - jnp.max(p, axis=0, keepdims=True))
        return e[0:1] / jnp.sum(e, axis=0, keepdims=True)

    lbs = (lower_bound(0), lower_bound(1))

    def gates(n, worst):
        rows = [chunk_rows(n * HG_UNROLL + u) for u in range(HG_UNROLL)]
        for d in range(2):
            terms = []
            for r in rows:
                k, g3 = _hg_gate_terms(logit_refs[d][r, :], lbs[d])
                k_ref[d, r, :] = k
                terms.append(g3)
            b_all = _dot(tri_ref[d], jnp.concatenate(terms, axis=1))
            for u, r in enumerate(rows):
                b = b_all[:, u * HG_D:(u + 1) * HG_D]
                b_ref[d, r, :] = b
                worst = jnp.minimum(worst, _hg_block_decay(b, bool(d)))
        return worst

    worst = lax.fori_loop(0, n_chunk // HG_UNROLL, gates, jnp.zeros((1, HG_D), F32))
    blocks_representable = jnp.min(worst) >= -HG_MAX_BLOCK_DECAY

    def within_chunks(blocked):
        def body(n, carry):
            lvl_masks = [lvl_ref[l] for l in range(lvl_ref.shape[0])]
            jobs = []
            for u in range(HG_UNROLL):
                ci = n * HG_UNROLL + u
                rows = chunk_rows(ci)
                for d, rev in enumerate((False, True)):
                    slot = d * HG_UNROLL + u
                    jobs.append(dict(
                        ci=ci, rows=rows, d=d, rev=rev, q=q_ref[rows, :].astype(F32),
                        k=k_ref[d, rows, :], b=b_ref[d, rows, :], v=i_ref[rows, :].astype(F32),
                        kpad=kpad_ref.at[slot], bpad=bpad_ref.at[slot], leaf_mask=leaf_ref[d],
                        block_mask=block_ref[d], spread=spread_ref[d]))
            for j, (o, q_in, st_add, dec) in zip(jobs, _hg_chunks(jobs, lvl_masks, blocked)):
                o_ref[j["d"], j["rows"], :] = o
                qin_ref[j["rows"], j["d"] * HG_D:(j["d"] + 1) * HG_D] = q_in
                add_ref[j["d"], j["ci"]] = st_add
                dec_ref[j["d"], j["ci"]] = dec
            return carry
        return body

    @pl.when(blocks_representable)
    def _():
        lax.fori_loop(0, n_chunk // HG_UNROLL, within_chunks(True), 0)

    @pl.when(jnp.logical_not(blocks_representable))
    def _():
        kpad_ref[...] = jnp.zeros_like(kpad_ref)
        bpad_ref[...] = jnp.zeros_like(bpad_ref)
        lax.fori_loop(0, n_chunk // HG_UNROLL, within_chunks(False), 0)

    def across_chunks(n, states):
        new_states = []
        for d, st in enumerate(states):
            ci = (n_chunk - 1 - n) if d else n
            st_ref[ci, :, d * HG_D:(d + 1) * HG_D] = st.astype(BF16)
            new_states.append(dec_ref[d, ci] * st + add_ref[d, ci])
        return tuple(new_states)

    zero_state = jnp.zeros((HG_D, HG_D), F32)
    lax.fori_loop(0, n_chunk, across_chunks, (zero_state, zero_state), unroll=True)

    def finish(n, carry):
        rows = chunk_rows(n)
        o = o_ref[0, rows, :] + o_ref[1, rows, :] + _dot_nt(qin_ref[rows, :], st_ref[n])
        y = _rms(o, gain_ref[...])
        gate = g_ref[rows, :].astype(F32)
        out_ref[rows, :] = (y * gate * _sigmoid(gate)).astype(BF16)
        return carry

    lax.fori_loop(0, n_chunk, finish, 0, unroll=8)


def _hgrn_stage(qig, fgate, lb_param, gain, batch, s_len):
    assert HG_CHUNK == LANES, "the leaf path rotates each row of a (C, C) tile by its row index"
    t_tok = qig.shape[1]
    c = HG_CHUNK
    r = jnp.arange(c)
    t_i, s_i = r[:, None], r[None, :]
    lower = s_i <= t_i
    causal = jnp.stack([lower, lower.T])
    tri = jnp.tile(causal, (1, 1, 3)).astype(BF16)
    lvl_masks = jnp.stack([t_i // (2 * h) == s_i // (2 * h) for h in _hg_levels(HG_LEAF)]).astype(F32)
    leaf_mask = (causal & (t_i // HG_LEAF == s_i // HG_LEAF)).astype(F32)
    block_mask = (causal & (t_i // HG_BLOCK == s_i // HG_BLOCK)).astype(F32)
    dist = jnp.repeat(jnp.arange(HG_LEAF), HG_D)[:, None]
    spread = jnp.stack([(-s_i) % HG_LEAF == dist, s_i % HG_LEAF == dist]).astype(BF16)
    blk = lambda col: pl.BlockSpec((None, s_len, HG_D),
                                   lambda b, h, col=col: (col * HG_HEADS + h, b, 0))
    consts = (tri, lvl_masks, leaf_mask, block_mask, spread)
    seq = lambda dtype: pltpu.VMEM((2, s_len, HG_D), dtype)
    return pl.pallas_call(
        _hgrn_kernel,
        grid=(batch, HG_HEADS),
        in_specs=[blk(0), blk(1), blk(2), blk(0), blk(1),
                  pl.BlockSpec((2, 2, HG_D), lambda b, h: (0, 0, h)),
                  pl.BlockSpec((1, HG_D), lambda b, h: (0, h))] + [_const_spec(t.shape) for t in consts],
        out_specs=pl.BlockSpec((None, s_len, HG_D), lambda b, h: (h, b, 0)),
        out_shape=jax.ShapeDtypeStruct((HG_HEADS, t_tok, HG_D), BF16),
        scratch_shapes=[seq(F32), seq(F32), seq(F32), pltpu.VMEM((s_len, 2 * HG_D), BF16),
                        pltpu.VMEM((2, s_len // c, HG_D, HG_D), F32),
                        pltpu.VMEM((2, s_len // c, 1, HG_D), F32),
                        pltpu.VMEM((s_len // c, HG_D, 2 * HG_D), BF16),
                        pltpu.VMEM((2 * HG_UNROLL, c + 2 * HG_LEAF, HG_D), F32),
                        pltpu.VMEM((2 * HG_UNROLL, c + 2 * HG_LEAF, HG_D), F32)],
        compiler_params=pltpu.CompilerParams(dimension_semantics=("arbitrary", "arbitrary"),
                                             vmem_limit_bytes=VMEM_LIMIT),
        name="hgrn_stage",
    )(qig, qig, qig, fgate, fgate, lb_param, gain, *consts)


def _attn_kernel(q_ref, k_ref, v_ref, o_ref):
    n_heads, s_len, _ = q_ref.shape
    sub = min(ATTN_SUB, s_len)
    items = [(h, slice(i * sub, (i + 1) * sub)) for h in range(n_heads) for i in range(s_len // sub)]
    scores = lambda h, rows: _dot_nt(q_ref[h, rows, :], k_ref[h])
    pending = [scores(*item) for item in items[:ATTN_DEPTH]]
    for n, (h, rows) in enumerate(items):
        s = pending.pop(0)
        if n + ATTN_DEPTH < len(items):
            pending.append(scores(*items[n + ATTN_DEPTH]))
        p = jnp.exp2(s - jnp.max(s, axis=-1, keepdims=True))
        o = _dot(p.astype(BF16), v_ref[h])
        o_ref[h, rows, :] = (o * (1.0 / jnp.sum(p, axis=-1, keepdims=True))).astype(BF16)


def _attn_stage(q_all, k_all, v, batch, s_len):
    t_tok = v.shape[1]
    n_grp = MLA_HEADS // ATTN_HEADS
    return pl.pallas_call(
        _attn_kernel,
        grid=(batch, n_grp),
        in_specs=[pl.BlockSpec((ATTN_HEADS, s_len, QK_PAD), lambda b, g: (g, b, 0)),
                  pl.BlockSpec((ATTN_HEADS, s_len, QK_PAD), lambda b, g: (g, b, 0)),
                  pl.BlockSpec((ATTN_HEADS, s_len, MLA_V), lambda b, g: (g, b, 0))],
        out_specs=pl.BlockSpec((ATTN_HEADS, s_len, MLA_V), lambda b, g: (g, b, 0)),
        out_shape=jax.ShapeDtypeStruct((MLA_HEADS, t_tok, MLA_V), BF16),
        compiler_params=pltpu.CompilerParams(dimension_semantics=("arbitrary",) * 2,
                                             vmem_limit_bytes=VMEM_LIMIT),
        name="attn_stage",
    )(q_all, k_all, v)


def _output_kernel(x_ref, a_ref, ob_ref, g_mla_ref, w_oa_ref, w_ob_ref, g_ffn_ref, w_up_ref,
                   w_dn_ref, out_ref):
    ob = _rms(_load_slabs(ob_ref).astype(F32), g_mla_ref[...]).astype(BF16)
    x1 = x_ref[...] + _dot(_load_slabs(a_ref), w_oa_ref[...]) + _dot(ob, w_ob_ref[...])
    h = _rms(x1, g_ffn_ref[...]).astype(BF16)
    up = jnp.maximum(_dot(h, w_up_ref[...]), 0.0)
    out_ref[...] = x1 + _dot((up * up).astype(BF16), w_dn_ref[...])


def _output_stage(x2, a, ob, g_mla, w_oa, w_ob, g_ffn, w_up, w_dn, tm):
    t_tok, d_model = x2.shape
    consts = (g_mla, w_oa, w_ob, g_ffn, w_up, w_dn)
    return pl.pallas_call(
        _output_kernel,
        grid=(t_tok // tm,),
        in_specs=[pl.BlockSpec((tm, d_model), lambda i: (i, 0)),
                  pl.BlockSpec((HG_HEADS, tm, HG_D), lambda i: (0, i, 0)),
                  pl.BlockSpec((MLA_HEADS, tm, MLA_V), lambda i: (0, i, 0))]
                 + [_const_spec(c.shape) for c in consts],
        out_specs=pl.BlockSpec((tm, d_model), lambda i: (i, 0)),
        out_shape=jax.ShapeDtypeStruct((t_tok, d_model), F32),
        compiler_params=pltpu.CompilerParams(dimension_semantics=("arbitrary",),
                                             vmem_limit_bytes=VMEM_LIMIT),
        name="output_stage",
    )(x2, a, ob, *consts)


def kernel(x, positions, g_mix_norm, w_in, lb_param, g_hgrn_out, g_cq, w_q_up, g_ckv, w_kv_up,
           g_q_norm, g_k_norm, g_mla_out, w_out, g_ffn_norm, w_up, w_down):
    batch, s_len, d_model = x.shape
    assert g_mix_norm.shape[0] == 1 and lb_param.shape[1] == 2, "one layer"
    assert s_len % (HG_CHUNK * HG_UNROLL) == 0
    t_tok = batch * s_len
    row = lambda t: t.reshape(1, -1).astype(F32)

    w = w_in[0]
    o_q, o_ff, o_fb, o_i, o_g, o_cq = (HG_W * n for n in range(6))
    o_ckv = o_cq + Q_LORA
    o_kr = o_ckv + KV_LORA
    w_qig = jnp.concatenate([w[:, o_q:o_ff], w[:, o_i:o_g], w[:, o_g:o_cq]], axis=1).astype(BF16)
    w_f = w[:, o_ff:o_i].astype(BF16)
    half = MLA_ROPE // 2
    swap = lambda t: jnp.concatenate([t[..., half:], t[..., :half]], axis=-1)
    pad_lanes = lambda t: jnp.pad(t, [(0, 0)] * (t.ndim - 1) + [(0, LANES - MLA_ROPE)])
    w_kr = w[:, o_kr:]
    w_mla = jnp.concatenate([w[:, o_cq:o_kr], pad_lanes(w_kr), pad_lanes(swap(w_kr))],
                            axis=1).astype(BF16)

    wq = w_q_up[0].reshape(Q_LORA, MLA_HEADS, MLA_QK)
    wq_rope = wq[:, :, MLA_NOPE:]
    w_qup = jnp.concatenate([wq[:, :, :MLA_NOPE].reshape(Q_LORA, -1),
                             pad_lanes(wq_rope).reshape(Q_LORA, -1),
                             pad_lanes(swap(wq_rope)).reshape(Q_LORA, -1)], axis=1).astype(BF16)
    wkv = w_kv_up[0].reshape(KV_LORA, MLA_HEADS, MLA_NOPE + MLA_V)
    w_kvup = jnp.concatenate([wkv[:, :, :MLA_NOPE].reshape(KV_LORA, -1),
                              wkv[:, :, MLA_NOPE:].reshape(KV_LORA, -1)], axis=1).astype(BF16)
    rope_gain = lambda t: jnp.stack([pad_lanes(t), pad_lanes(swap(t))]).astype(F32)
    g_qn, g_qr = row(g_q_norm[0, :MLA_NOPE]), rope_gain(g_q_norm[0, MLA_NOPE:])
    g_kn, g_kr = row(g_k_norm[0, :MLA_NOPE]), rope_gain(g_k_norm[0, MLA_NOPE:])
    invf = (ROPE_THETA ** (-jnp.arange(0, MLA_ROPE, 2, dtype=F32) / MLA_ROPE)).reshape(-1, 1)

    x2 = x.reshape(t_tok, d_model)
    pos = positions.reshape(1, t_tok).astype(F32)
    tm = min(512, s_len)
    qig, fgate, q_all, k_all, v = _input_stage(
        x2, pos, invf, row(g_mix_norm[0]), w_qig, w_f, w_mla, row(g_cq[0]), w_qup, row(g_ckv[0]),
        w_kvup, g_qn, g_qr, g_kn, g_kr, min(INPUT_TILE, s_len))

    a = _hgrn_stage(qig, fgate, lb_param.astype(F32), g_hgrn_out[0].reshape(1, HG_W).astype(F32),
                    batch, s_len)
    ob = _attn_stage(q_all, k_all, v, batch, s_len)

    wo = w_out[0].astype(BF16)
    out = _output_stage(x2, a, ob, row(g_mla_out[0]), wo[:HG_W], wo[HG_W:], row(g_ffn_norm[0]),
                        w_up[0].astype(BF16), w_down[0].astype(BF16), tm)
    return out.reshape(batch, s_len, d_model)
```

```python
import functools

import jax
import jax.numpy as jnp
from jax import lax
from jax.experimental import pallas as pl
from jax.experimental.pallas import tpu as pltpu

NORM_EPS = 1e-6
HG_HEADS = 4
HG_D = 128
HG_W = HG_HEADS * HG_D
MLA_HEADS = 4
MLA_NOPE = 128
MLA_ROPE = 64
MLA_V = 128
MLA_QK = MLA_NOPE + MLA_ROPE
Q_LORA = 384
KV_LORA = 256
ROPE_THETA = 10000.0
LANES = 128
QK_PAD = 2 * LANES

HG_CHUNK = 128
HG_LEAF = 8
HG_BLOCK = 64
HG_MAX_BLOCK_DECAY = 100.0
HG_UNROLL = 8
ATTN_SUB = 512
ATTN_HEADS = 2
ATTN_DEPTH = 1
INPUT_TILE = 512
INPUT_SUB = 256
LOG2_E = 1.4426950408889634
VMEM_LIMIT = 56 * 1024 * 1024

F32 = jnp.float32
BF16 = jnp.bfloat16


def _dot(a, b):
    return jnp.dot(a, b, preferred_element_type=F32)


def _dot_nt(a, b):
    return lax.dot_general(a, b, (((1,), (1,)), ((), ())), preferred_element_type=F32)


def _rms(t, gain):
    return t * lax.rsqrt(jnp.mean(t * t, axis=-1, keepdims=True) + NORM_EPS) * gain


def _sigmoid(t):
    return 1.0 / (1.0 + jnp.exp2(t * -LOG2_E))


def _store_slabs(ref, rows, t):
    for j in range(ref.shape[0]):
        ref[j, rows, :] = t[:, j * LANES:(j + 1) * LANES].astype(ref.dtype)


def _load_slabs(ref):
    return jnp.concatenate([ref[j] for j in range(ref.shape[0])], axis=1)


def _input_kernel(x_ref, pos_ref, invf_ref, g_mix_ref, w_qig_ref, w_f_ref, w_mla_ref, g_cq_ref,
                  w_qup_ref, g_ckv_ref, w_kvup_ref, g_qn_ref, g_qr_ref, g_kn_ref, g_kr_ref,
                  qig_ref, f_ref, qa_ref, ka_ref, v_ref):
    tm = x_ref.shape[0]
    sub = min(INPUT_SUB, tm)
    tiles = [dict(rows=slice(i * sub, (i + 1) * sub)) for i in range(tm // sub)]
    scale = MLA_QK ** -0.5 * LOG2_E

    def project(t):
        h = _rms(x_ref[t["rows"], :], g_mix_ref[...]).astype(BF16)
        _store_slabs(qig_ref, t["rows"], _dot(h, w_qig_ref[...]))
        _store_slabs(f_ref, t["rows"], _dot(h, w_f_ref[...]))
        t["m"] = _dot(h, w_mla_ref[...])

    def up_project(t):
        c_q = t["m"][:, :Q_LORA]
        c_kv = t["m"][:, Q_LORA:Q_LORA + KV_LORA]
        t["q"] = _dot(_rms(c_q, g_cq_ref[...]).astype(BF16), w_qup_ref[...])
        t["kv"] = _dot(_rms(c_kv, g_ckv_ref[...]).astype(BF16), w_kvup_ref[...])

    def rope_tables():
        ang = invf_ref[...] * pos_ref[...]
        c, s = jnp.cos(ang), jnp.sin(ang)
        z = jnp.zeros_like(c)
        return jnp.concatenate([c, c, z, z], axis=0).T, jnp.concatenate([-s, s, z, z], axis=0).T

    def heads(t, cos_t, sin_t):
        rows, q, kv = t["rows"], t["q"], t["kv"]
        _store_slabs(v_ref, rows, kv[:, MLA_HEADS * MLA_NOPE:])
        cos_r, sin_r = cos_t[rows, :], sin_t[rows, :]
        q_cos, q_sin = g_qr_ref[0:1] * cos_r, g_qr_ref[1:2] * sin_r
        rope_cols = Q_LORA + KV_LORA
        k_r = t["m"][:, rope_cols:rope_cols + LANES]
        k_pe = (k_r * (g_kr_ref[0:1] * cos_r)
                + t["m"][:, rope_cols + LANES:] * (g_kr_ref[1:2] * sin_r))
        sq_kr = k_r * k_r
        for hd in range(MLA_HEADS):
            lo, hi = hd * LANES, (hd + 1) * LANES
            q_n = q[:, lo:hi]
            q_r = q[:, MLA_HEADS * MLA_NOPE + lo:MLA_HEADS * MLA_NOPE + hi]
            q_r_swapped = q[:, 2 * MLA_HEADS * MLA_NOPE + lo:2 * MLA_HEADS * MLA_NOPE + hi]
            ssq = jnp.sum(q_n * q_n + q_r * q_r, axis=-1, keepdims=True)
            rstd = lax.rsqrt(ssq * (1.0 / MLA_QK) + NORM_EPS) * scale
            q_pe = q_r * q_cos + q_r_swapped * q_sin
            qa_ref[hd, rows, :] = jnp.concatenate([q_n * g_qn_ref[...] * rstd, q_pe * rstd],
                                                  axis=1).astype(BF16)
            k_n = kv[:, lo:hi]
            ssq = jnp.sum(k_n * k_n + sq_kr, axis=-1, keepdims=True)
            rstd = lax.rsqrt(ssq * (1.0 / MLA_QK) + NORM_EPS)
            ka_ref[hd, rows, :] = jnp.concatenate([k_n * g_kn_ref[...] * rstd, k_pe * rstd],
                                                  axis=1).astype(BF16)

    tables = None
    for i, t in enumerate(tiles):
        project(t)
        if i == 0:
            tables = rope_tables()
        else:
            heads(tiles[i - 1], *tables)
        up_project(t)
    heads(tiles[-1], *tables)


def _const_spec(shape):
    nd = len(shape)
    return pl.BlockSpec(shape, lambda *_: (0,) * nd, pipeline_mode=pl.Buffered(1))


def _input_stage(x2, pos, invf, g_mix, w_qig, w_f, w_mla, g_cq, w_qup, g_ckv, w_kvup,
                 g_qn, g_qr, g_kn, g_kr, tm):
    t_tok, d_model = x2.shape
    consts = (invf, g_mix, w_qig, w_f, w_mla, g_cq, w_qup, g_ckv, w_kvup, g_qn, g_qr, g_kn, g_kr)
    return pl.pallas_call(
        _input_kernel,
        grid=(t_tok // tm,),
        in_specs=[pl.BlockSpec((tm, d_model), lambda i: (i, 0)),
                  pl.BlockSpec((1, tm), lambda i: (0, i))] + [_const_spec(c.shape) for c in consts],
        out_specs=[pl.BlockSpec((3 * HG_HEADS, tm, HG_D), lambda i: (0, i, 0)),
                   pl.BlockSpec((2 * HG_HEADS, tm, HG_D), lambda i: (0, i, 0)),
                   pl.BlockSpec((MLA_HEADS, tm, QK_PAD), lambda i: (0, i, 0)),
                   pl.BlockSpec((MLA_HEADS, tm, QK_PAD), lambda i: (0, i, 0)),
                   pl.BlockSpec((MLA_HEADS, tm, MLA_V), lambda i: (0, i, 0))],
        out_shape=[jax.ShapeDtypeStruct((3 * HG_HEADS, t_tok, HG_D), BF16),
                   jax.ShapeDtypeStruct((2 * HG_HEADS, t_tok, HG_D), F32),
                   jax.ShapeDtypeStruct((MLA_HEADS, t_tok, QK_PAD), BF16),
                   jax.ShapeDtypeStruct((MLA_HEADS, t_tok, QK_PAD), BF16),
                   jax.ShapeDtypeStruct((MLA_HEADS, t_tok, MLA_V), BF16)],
        compiler_params=pltpu.CompilerParams(dimension_semantics=("arbitrary",),
                                             vmem_limit_bytes=VMEM_LIMIT),
        name="input_stage",
    )(x2, pos, *consts)


def _hg_levels(leaf):
    sizes = []
    half = HG_CHUNK // 2
    while half >= leaf:
        sizes.append(half)
        half //= 2
    return tuple(sizes)


def _hg_gate_terms(logit, lb):
    f = lb + (1.0 - lb) * _sigmoid(logit)
    g = jnp.log2(f)
    g_hi = g.astype(BF16)
    r1 = g - g_hi.astype(F32)
    g_mid = r1.astype(BF16)
    g_lo = (r1 - g_mid.astype(F32)).astype(BF16)
    return 1.0 - f, jnp.concatenate([g_hi, g_mid, g_lo], axis=0)


def _hg_block_decay(b, rev):
    n_blk = HG_CHUNK // HG_BLOCK
    edge = [b[i * HG_BLOCK:i * HG_BLOCK + 1] if rev else b[(i + 1) * HG_BLOCK - 1:(i + 1) * HG_BLOCK]
            for i in range(n_blk)]
    worst = edge[-1] if rev else edge[0]
    for i in range(1, n_blk):
        worst = jnp.minimum(worst, edge[i - 1] - edge[i] if rev else edge[i] - edge[i - 1])
    return worst


def _hg_pairs_blocked(jobs):
    c, n_blk = HG_CHUNK, HG_CHUNK // HG_BLOCK
    shape3 = (n_blk, HG_BLOCK, HG_D)
    zero_row = jnp.zeros((1, 1, HG_D), F32)
    zero_blk = jnp.zeros((HG_BLOCK, HG_D), F32)
    rows_bf16 = lambda blocks: jnp.concatenate(blocks, axis=0).astype(BF16)
    halves = [h // HG_BLOCK for h in _hg_levels(HG_BLOCK)]
    for j in jobs:
        rev = j["rev"]
        b3 = j["b"].reshape(shape3)
        if rev:
            end = b3[:, 0:1]
            ref = jnp.concatenate([end[1:], zero_row], axis=0)
        else:
            end = b3[:, HG_BLOCK - 1:HG_BLOCK]
            ref = jnp.concatenate([zero_row, end[:-1]], axis=0)
        order = [n_blk - 1 - i for i in range(n_blk)] if rev else list(range(n_blk))
        block_at = {o: i for i, o in enumerate(order)}
        b_tot = end[block_at[n_blk - 1]]
        q_b = j["q"].reshape(shape3) * jnp.exp2(b3 - ref)
        k_f = j["k"].reshape(shape3) * jnp.exp2(end - b3)
        j["dec"] = jnp.exp2(b_tot)
        j["q_in"] = rows_bf16([q_b[i] * jnp.exp2(ref[i]) for i in range(n_blk)])
        j["k_out"] = rows_bf16([k_f[i] * jnp.exp2(b_tot - end[i]) for i in range(n_blk)])
        k_blk = rows_bf16([k_f[i] * jnp.exp2(ref[i] - end[i]) for i in range(n_blk)])
        j["p"] = _dot_nt(q_b.reshape(c, HG_D).astype(BF16), k_blk) * j["block_mask"]
        q_x, k_x = [], []
        for half in halves:
            for first in range(0, n_blk, 2 * half):
                mid = end[block_at[first + half - 1]]
                q_l, k_l = [], []
                for i, o in enumerate(order):
                    late = first + half <= o < first + 2 * half
                    early = first <= o < first + half
                    q_l.append(zero_blk if not late else q_b[i] if o == first + half
                               else q_b[i] * jnp.exp2(ref[i] - mid))
                    k_l.append(zero_blk if not early else k_f[i] if o == first + half - 1
                               else k_f[i] * jnp.exp2(mid - end[i]))
                q_x.append(rows_bf16(q_l))
                k_x.append(rows_bf16(k_l))
        j["cross"] = (jnp.concatenate(q_x, axis=1), jnp.concatenate(k_x, axis=1))

    for j in jobs:
        j["p"] = j["p"] + _dot_nt(*j["cross"])

    for fwd, bwd in zip(jobs[0::2], jobs[1::2]):
        st_add = _dot(fwd["v"].T.astype(BF16), jnp.concatenate([fwd["k_out"], bwd["k_out"]], axis=1))
        fwd["st_add"], bwd["st_add"] = st_add[:, :HG_D], st_add[:, HG_D:]
    for fwd, bwd in zip(jobs[0::2], jobs[1::2]):
        p = jnp.concatenate([fwd["p"], bwd["p"]], axis=0).astype(BF16)
        o = _dot(p, fwd["v"].astype(BF16))
        fwd["o"], bwd["o"] = o[:c], o[c:]


def _hg_chunks(jobs, lvl_masks, blocked):
    c = HG_CHUNK
    if blocked:
        _hg_pairs_blocked(jobs)
        return [(j["o"], j["q_in"], j["st_add"], j["dec"]) for j in jobs]

    for j in jobs:
        q, k, b = j["q"], j["k"], j["b"]
        b_tot = b[0:1] if j["rev"] else b[c - 1:c]
        j["dec"] = jnp.exp2(b_tot)
        j["q_in"] = (q * jnp.exp2(b)).astype(BF16)
        k_out = (k * jnp.exp2(b_tot - b)).astype(BF16)
        j["st_add"] = _dot(j["v"].T.astype(BF16), k_out)
        j["p"] = None

    for lvl, half in enumerate(_hg_levels(HG_LEAF)):
        n_grp = c // (2 * half)
        shape3 = (n_grp, 2 * half, HG_D)
        zero = jnp.zeros((n_grp, half, HG_D), F32)
        for j in jobs:
            b3, q3, k3 = j["b"].reshape(shape3), j["q"].reshape(shape3), j["k"].reshape(shape3)
            if j["rev"]:
                mid = b3[:, half:half + 1]
                q_l = jnp.concatenate([q3[:, :half] * jnp.exp2(b3[:, :half] - mid), zero], axis=1)
                k_l = jnp.concatenate([zero, k3[:, half:] * jnp.exp2(mid - b3[:, half:])], axis=1)
            else:
                mid = b3[:, half - 1:half]
                q_l = jnp.concatenate([zero, q3[:, half:] * jnp.exp2(b3[:, half:] - mid)], axis=1)
                k_l = jnp.concatenate([k3[:, :half] * jnp.exp2(mid - b3[:, :half]), zero], axis=1)
            p_l = _dot_nt(q_l.reshape(c, HG_D).astype(BF16), k_l.reshape(c, HG_D).astype(BF16))
            p_l = p_l * lvl_masks[lvl] if n_grp > 1 else p_l
            j["p"] = p_l if j["p"] is None else j["p"] + p_l

    for j in jobs:
        j["kpad"][HG_LEAF:HG_LEAF + c, :] = j["k"]
        j["bpad"][HG_LEAF:HG_LEAF + c, :] = j["b"]
    for j in jobs:
        w = [(j["q"] * j["k"]).astype(BF16)]
        for d in range(1, HG_LEAF):
            lo = HG_LEAF + d if j["rev"] else HG_LEAF - d
            k_s = j["kpad"][lo:lo + c, :]
            b_s = j["bpad"][lo:lo + c, :]
            w.append((j["q"] * k_s * jnp.exp2(j["b"] - b_s)).astype(BF16))
        j["row_sums"] = _dot(jnp.concatenate(w, axis=1), j["spread"])
    for j in jobs:
        j["p_leaf"] = (pltpu.roll(j["row_sums"], 0, 1, stride=1, stride_axis=0)
                       * j["leaf_mask"])

    outs = []
    for j in jobs:
        p = (j["p"] + j["p_leaf"]).astype(BF16)
        outs.append((_dot(p, j["v"].astype(BF16)), j["q_in"], j["st_add"], j["dec"]))
    return outs


def _hgrn_kernel(q_ref, i_ref, g_ref, ff_ref, fb_ref, lbp_ref, gain_ref, tri_ref, lvl_ref, leaf_ref,
                 block_ref, spread_ref, out_ref, k_ref, b_ref, o_ref, qin_ref, add_ref, dec_ref,
                 st_ref, kpad_ref, bpad_ref):
    s_len = q_ref.shape[0]
    n_chunk = s_len // HG_CHUNK
    logit_refs = (ff_ref, fb_ref)
    chunk_rows = lambda ci: pl.ds(pl.multiple_of(ci * HG_CHUNK, HG_CHUNK), HG_CHUNK)

    def lower_bound(d):
        p = lbp_ref[d]
        e = jnp.exp(p - jnp.max(p, axis=0, keepdims=True))
        return e[0:1] / jnp.sum(e, axis=0, keepdims=True)

    lbs = (lower_bound(0), lower_bound(1))

    def gates(n, worst):
        rows = [chunk_rows(n * HG_UNROLL + u) for u in range(HG_UNROLL)]
        for d in range(2):
            terms = []
            for r in rows:
                k, g3 = _hg_gate_terms(logit_refs[d][r, :], lbs[d])
                k_ref[d, r, :] = k
                terms.append(g3)
            b_all = _dot(tri_ref[d], jnp.concatenate(terms, axis=1))
            for u, r in enumerate(rows):
                b = b_all[:, u * HG_D:(u + 1) * HG_D]
                b_ref[d, r, :] = b
                worst = jnp.minimum(worst, _hg_block_decay(b, bool(d)))
        return worst

    worst = lax.fori_loop(0, n_chunk // HG_UNROLL, gates, jnp.zeros((1, HG_D), F32))
    blocks_representable = jnp.min(worst) >= -HG_MAX_BLOCK_DECAY

    def within_chunks(blocked):
        def body(n, carry):
            lvl_masks = [lvl_ref[l] for l in range(lvl_ref.shape[0])]
            jobs = []
            for u in range(HG_UNROLL):
                ci = n * HG_UNROLL + u
                rows = chunk_rows(ci)
                for d, rev in enumerate((False, True)):
                    slot = d * HG_UNROLL + u
                    jobs.append(dict(
                        ci=ci, rows=rows, d=d, rev=rev, q=q_ref[rows, :].astype(F32),
                        k=k_ref[d, rows, :], b=b_ref[d, rows, :], v=i_ref[rows, :].astype(F32),
                        kpad=kpad_ref.at[slot], bpad=bpad_ref.at[slot], leaf_mask=leaf_ref[d],
                        block_mask=block_ref[d], spread=spread_ref[d]))
            for j, (o, q_in, st_add, dec) in zip(jobs, _hg_chunks(jobs, lvl_masks, blocked)):
                o_ref[j["d"], j["rows"], :] = o
                qin_ref[j["rows"], j["d"] * HG_D:(j["d"] + 1) * HG_D] = q_in
                add_ref[j["d"], j["ci"]] = st_add
                dec_ref[j["d"], j["ci"]] = dec
            return carry
        return body

    @pl.when(blocks_representable)
    def _():
        lax.fori_loop(0, n_chunk // HG_UNROLL, within_chunks(True), 0)

    @pl.when(jnp.logical_not(blocks_representable))
    def _():
        kpad_ref[...] = jnp.zeros_like(kpad_ref)
        bpad_ref[...] = jnp.zeros_like(bpad_ref)
        lax.fori_loop(0, n_chunk // HG_UNROLL, within_chunks(False), 0)

    def across_chunks(n, states):
        new_states = []
        for d, st in enumerate(states):
            ci = (n_chunk - 1 - n) if d else n
            st_ref[ci, :, d * HG_D:(d + 1) * HG_D] = st.astype(BF16)
            new_states.append(dec_ref[d, ci] * st + add_ref[d, ci])
        return tuple(new_states)

    zero_state = jnp.zeros((HG_D, HG_D), F32)
    lax.fori_loop(0, n_chunk, across_chunks, (zero_state, zero_state), unroll=True)

    def finish(n, carry):
        rows = chunk_rows(n)
        o = o_ref[0, rows, :] + o_ref[1, rows, :] + _dot_nt(qin_ref[rows, :], st_ref[n])
        y = _rms(o, gain_ref[...])
        gate = g_ref[rows, :].astype(F32)
        out_ref[rows, :] = (y * gate * _sigmoid(gate)).astype(BF16)
        return carry

    lax.fori_loop(0, n_chunk, finish, 0, unroll=8)


def _hgrn_stage(qig, fgate, lb_param, gain, batch, s_len):
    assert HG_CHUNK == LANES, "the leaf path rotates each row of a (C, C) tile by its row index"
    t_tok = qig.shape[1]
    c = HG_CHUNK
    r = jnp.arange(c)
    t_i, s_i = r[:, None], r[None, :]
    lower = s_i <= t_i
    causal = jnp.stack([lower, lower.T])
    tri = jnp.tile(causal, (1, 1, 3)).astype(BF16)
    lvl_masks = jnp.stack([t_i // (2 * h) == s_i // (2 * h) for h in _hg_levels(HG_LEAF)]).astype(F32)
    leaf_mask = (causal & (t_i // HG_LEAF == s_i // HG_LEAF)).astype(F32)
    block_mask = (causal & (t_i // HG_BLOCK == s_i // HG_BLOCK)).astype(F32)
    dist = jnp.repeat(jnp.arange(HG_LEAF), HG_D)[:, None]
    spread = jnp.stack([(-s_i) % HG_LEAF == dist, s_i % HG_LEAF == dist]).astype(BF16)
    blk = lambda col: pl.BlockSpec((None, s_len, HG_D),
                                   lambda b, h, col=col: (col * HG_HEADS + h, b, 0))
    consts = (tri, lvl_masks, leaf_mask, block_mask, spread)
    seq = lambda dtype: pltpu.VMEM((2, s_len, HG_D), dtype)
    return pl.pallas_call(
        _hgrn_kernel,
        grid=(batch, HG_HEADS),
        in_specs=[blk(0), blk(1), blk(2), blk(0), blk(1),
                  pl.BlockSpec((2, 2, HG_D), lambda b, h: (0, 0, h)),
                  pl.BlockSpec((1, HG_D), lambda b, h: (0, h))] + [_const_spec(t.shape) for t in consts],
        out_specs=pl.BlockSpec((None, s_len, HG_D), lambda b, h: (h, b, 0)),
        out_shape=jax.ShapeDtypeStruct((HG_HEADS, t_tok, HG_D), BF16),
        scratch_shapes=[seq(F32), seq(F32), seq(F32), pltpu.VMEM((s_len, 2 * HG_D), BF16),
                        pltpu.VMEM((2, s_len // c, HG_D, HG_D), F32),
                        pltpu.VMEM((2, s_len // c, 1, HG_D), F32),
                        pltpu.VMEM((s_len // c, HG_D, 2 * HG_D), BF16),
                        pltpu.VMEM((2 * HG_UNROLL, c + 2 * HG_LEAF, HG_D), F32),
                        pltpu.VMEM((2 * HG_UNROLL, c + 2 * HG_LEAF, HG_D), F32)],
        compiler_params=pltpu.CompilerParams(dimension_semantics=("arbitrary", "arbitrary"),
                                             vmem_limit_bytes=VMEM_LIMIT),
        name="hgrn_stage",
    )(qig, qig, qig, fgate, fgate, lb_param, gain, *consts)


def _attn_kernel(q_ref, k_ref, v_ref, o_ref):
    n_heads, s_len, _ = q_ref.shape
    sub = min(ATTN_SUB, s_len)
    items = [(h, slice(i * sub, (i + 1) * sub)) for h in range(n_heads) for i in range(s_len // sub)]
    scores = lambda h, rows: _dot_nt(q_ref[h, rows, :], k_ref[h])
    pending = [scores(*item) for item in items[:ATTN_DEPTH]]
    for n, (h, rows) in enumerate(items):
        s = pending.pop(0)
        if n + ATTN_DEPTH < len(items):
            pending.append(scores(*items[n + ATTN_DEPTH]))
        p = jnp.exp2(s - jnp.max(s, axis=-1, keepdims=True))
        o = _dot(p.astype(BF16), v_ref[h])
        o_ref[h, rows, :] = (o * (1.0 / jnp.sum(p, axis=-1, keepdims=True))).astype(BF16)


def _attn_stage(q_all, k_all, v, batch, s_len):
    t_tok = v.shape[1]
    n_grp = MLA_HEADS // ATTN_HEADS
    return pl.pallas_call(
        _attn_kernel,
        grid=(batch, n_grp),
        in_specs=[pl.BlockSpec((ATTN_HEADS, s_len, QK_PAD), lambda b, g: (g, b, 0)),
                  pl.BlockSpec((ATTN_HEADS, s_len, QK_PAD), lambda b, g: (g, b, 0)),
                  pl.BlockSpec((ATTN_HEADS, s_len, MLA_V), lambda b, g: (g, b, 0))],
        out_specs=pl.BlockSpec((ATTN_HEADS, s_len, MLA_V), lambda b, g: (g, b, 0)),
        out_shape=jax.ShapeDtypeStruct((MLA_HEADS, t_tok, MLA_V), BF16),
        compiler_params=pltpu.CompilerParams(dimension_semantics=("arbitrary",) * 2,
                                             vmem_limit_bytes=VMEM_LIMIT),
        name="attn_stage",
    )(q_all, k_all, v)


def _output_kernel(x_ref, a_ref, ob_ref, g_mla_ref, w_oa_ref, w_ob_ref, g_ffn_ref, w_up_ref,
                   w_dn_ref, out_ref):
    ob = _rms(_load_slabs(ob_ref).astype(F32), g_mla_ref[...]).astype(BF16)
    x1 = x_ref[...] + _dot(_load_slabs(a_ref), w_oa_ref[...]) + _dot(ob, w_ob_ref[...])
    h = _rms(x1, g_ffn_ref[...]).astype(BF16)
    up = jnp.maximum(_dot(h, w_up_ref[...]), 0.0)
    out_ref[...] = x1 + _dot((up * up).astype(BF16), w_dn_ref[...])


def _output_stage(x2, a, ob, g_mla, w_oa, w_ob, g_ffn, w_up, w_dn, tm):
    t_tok, d_model = x2.shape
    consts = (g_mla, w_oa, w_ob, g_ffn, w_up, w_dn)
    return pl.pallas_call(
        _output_kernel,
        grid=(t_tok // tm,),
        in_specs=[pl.BlockSpec((tm, d_model), lambda i: (i, 0)),
                  pl.BlockSpec((HG_HEADS, tm, HG_D), lambda i: (0, i, 0)),
                  pl.BlockSpec((MLA_HEADS, tm, MLA_V), lambda i: (0, i, 0))]
                 + [_const_spec(c.shape) for c in consts],
        out_specs=pl.BlockSpec((tm, d_model), lambda i: (i, 0)),
        out_shape=jax.ShapeDtypeStruct((t_tok, d_model), F32),
        compiler_params=pltpu.CompilerParams(dimension_semantics=("arbitrary",),
                                             vmem_limit_bytes=VMEM_LIMIT),
        name="output_stage",
    )(x2, a, ob, *consts)


def kernel(x, positions, g_mix_norm, w_in, lb_param, g_hgrn_out, g_cq, w_q_up, g_ckv, w_kv_up,
           g_q_norm, g_k_norm, g_mla_out, w_out, g_ffn_norm, w_up, w_down):
    batch, s_len, d_model = x.shape
    assert g_mix_norm.shape[0] == 1 and lb_param.shape[1] == 2, "one layer"
    assert s_len % (HG_CHUNK * HG_UNROLL) == 0
    t_tok = batch * s_len
    row = lambda t: t.reshape(1, -1).astype(F32)

    w = w_in[0]
    o_q, o_ff, o_fb, o_i, o_g, o_cq = (HG_W * n for n in range(6))
    o_ckv = o_cq + Q_LORA
    o_kr = o_ckv + KV_LORA
    w_qig = jnp.concatenate([w[:, o_q:o_ff], w[:, o_i:o_g], w[:, o_g:o_cq]], axis=1).astype(BF16)
    w_f = w[:, o_ff:o_i].astype(BF16)
    half = MLA_ROPE // 2
    swap = lambda t: jnp.concatenate([t[..., half:], t[..., :half]], axis=-1)
    pad_lanes = lambda t: jnp.pad(t, [(0, 0)] * (t.ndim - 1) + [(0, LANES - MLA_ROPE)])
    w_kr = w[:, o_kr:]
    w_mla = jnp.concatenate([w[:, o_cq:o_kr], pad_lanes(w_kr), pad_lanes(swap(w_kr))],
                            axis=1).astype(BF16)

    wq = w_q_up[0].reshape(Q_LORA, MLA_HEADS, MLA_QK)
    wq_rope = wq[:, :, MLA_NOPE:]
    w_qup = jnp.concatenate([wq[:, :, :MLA_NOPE].reshape(Q_LORA, -1),
                             pad_lanes(wq_rope).reshape(Q_LORA, -1),
                             pad_lanes(swap(wq_rope)).reshape(Q_LORA, -1)], axis=1).astype(BF16)
    wkv = w_kv_up[0].reshape(KV_LORA, MLA_HEADS, MLA_NOPE + MLA_V)
    w_kvup = jnp.concatenate([wkv[:, :, :MLA_NOPE].reshape(KV_LORA, -1),
                              wkv[:, :, MLA_NOPE:].reshape(KV_LORA, -1)], axis=1).astype(BF16)
    rope_gain = lambda t: jnp.stack([pad_lanes(t), pad_lanes(swap(t))]).astype(F32)
    g_qn, g_qr = row(g_q_norm[0, :MLA_NOPE]), rope_gain(g_q_norm[0, MLA_NOPE:])
    g_kn, g_kr = row(g_k_norm[0, :MLA_NOPE]), rope_gain(g_k_norm[0, MLA_NOPE:])
    invf = (ROPE_THETA ** (-jnp.arange(0, MLA_ROPE, 2, dtype=F32) / MLA_ROPE)).reshape(-1, 1)

    x2 = x.reshape(t_tok, d_model)
    pos = positions.reshape(1, t_tok).astype(F32)
    tm = min(512, s_len)
    qig, fgate, q_all, k_all, v = _input_stage(
        x2, pos, invf, row(g_mix_norm[0]), w_qig, w_f, w_mla, row(g_cq[0]), w_qup, row(g_ckv[0]),
        w_kvup, g_qn, g_qr, g_kn, g_kr, min(INPUT_TILE, s_len))

    a = _hgrn_stage(qig, fgate, lb_param.astype(F32), g_hgrn_out[0].reshape(1, HG_W).astype(F32),
                    batch, s_len)
    ob = _attn_stage(q_all, k_all, v, batch, s_len)

    wo = w_out[0].astype(BF16)
    out = _output_stage(x2, a, ob, row(g_mla_out[0]), wo[:HG_W], wo[HG_W:], row(g_ffn_norm[0]),
                        w_up[0].astype(BF16), w_down[0].astype(BF16), tm)
    return out.reshape(batch, s_len, d_model)
```

```python
import functools

import jax
import jax.numpy as jnp
from jax import lax
from jax.experimental import pallas as pl
from jax.experimental.pallas import tpu as pltpu

NORM_EPS = 1e-6
HG_HEADS = 4
HG_D = 128
HG_W = HG_HEADS * HG_D
MLA_HEADS = 4
MLA_NOPE = 128
MLA_ROPE = 64
MLA_V = 128
MLA_QK = MLA_NOPE + MLA_ROPE
Q_LORA = 384
KV_LORA = 256
ROPE_THETA = 10000.0
LANES = 128
QK_PAD = 2 * LANES

HG_CHUNK = 128
HG_LEAF = 8
HG_BLOCK = 64
HG_MAX_BLOCK_DECAY = 100.0
HG_UNROLL = 8
ATTN_SUB = 512
ATTN_HEADS = 2
ATTN_DEPTH = 1
INPUT_TILE = 512
INPUT_SUB = 256
LOG2_E = 1.4426950408889634
VMEM_LIMIT = 56 * 1024 * 1024

F32 = jnp.float32
BF16 = jnp.bfloat16


def _dot(a, b):
    return jnp.dot(a, b, preferred_element_type=F32)


def _dot_nt(a, b):
    return lax.dot_general(a, b, (((1,), (1,)), ((), ())), preferred_element_type=F32)


def _rms(t, gain):
    return t * lax.rsqrt(jnp.mean(t * t, axis=-1, keepdims=True) + NORM_EPS) * gain


def _sigmoid(t):
    return 1.0 / (1.0 + jnp.exp2(t * -LOG2_E))


def _store_slabs(ref, rows, t):
    for j in range(ref.shape[0]):
        ref[j, rows, :] = t[:, j * LANES:(j + 1) * LANES].astype(ref.dtype)


def _load_slabs(ref):
    return jnp.concatenate([ref[j] for j in range(ref.shape[0])], axis=1)


def _input_kernel(x_ref, pos_ref, invf_ref, g_mix_ref, w_qig_ref, w_f_ref, w_mla_ref, g_cq_ref,
                  w_qup_ref, g_ckv_ref, w_kvup_ref, g_qn_ref, g_qr_ref, g_kn_ref, g_kr_ref,
                  qig_ref, f_ref, qa_ref, ka_ref, v_ref):
    tm = x_ref.shape[0]
    sub = min(INPUT_SUB, tm)
    tiles = [dict(rows=slice(i * sub, (i + 1) * sub)) for i in range(tm // sub)]
    scale = MLA_QK ** -0.5 * LOG2_E

    def project(t):
        h = _rms(x_ref[t["rows"], :], g_mix_ref[...]).astype(BF16)
        _store_slabs(qig_ref, t["rows"], _dot(h, w_qig_ref[...]))
        _store_slabs(f_ref, t["rows"], _dot(h, w_f_ref[...]))
        t["m"] = _dot(h, w_mla_ref[...])

    def up_project(t):
        c_q = t["m"][:, :Q_LORA]
        c_kv = t["m"][:, Q_LORA:Q_LORA + KV_LORA]
        t["q"] = _dot(_rms(c_q, g_cq_ref[...]).astype(BF16), w_qup_ref[...])
        t["kv"] = _dot(_rms(c_kv, g_ckv_ref[...]).astype(BF16), w_kvup_ref[...])

    def rope_tables():
        ang = invf_ref[...] * pos_ref[...]
        c, s = jnp.cos(ang), jnp.sin(ang)
        z = jnp.zeros_like(c)
        return jnp.concatenate([c, c, z, z], axis=0).T, jnp.concatenate([-s, s, z, z], axis=0).T

    def heads(t, cos_t, sin_t):
        rows, q, kv = t["rows"], t["q"], t["kv"]
        _store_slabs(v_ref, rows, kv[:, MLA_HEADS * MLA_NOPE:])
        cos_r, sin_r = cos_t[rows, :], sin_t[rows, :]
        q_cos, q_sin = g_qr_ref[0:1] * cos_r, g_qr_ref[1:2] * sin_r
        rope_cols = Q_LORA + KV_LORA
        k_r = t["m"][:, rope_cols:rope_cols + LANES]
        k_pe = (k_r * (g_kr_ref[0:1] * cos_r)
                + t["m"][:, rope_cols + LANES:] * (g_kr_ref[1:2] * sin_r))
        sq_kr = k_r * k_r
        for hd in range(MLA_HEADS):
            lo, hi = hd * LANES, (hd + 1) * LANES
            q_n = q[:, lo:hi]
            q_r = q[:, MLA_HEADS * MLA_NOPE + lo:MLA_HEADS * MLA_NOPE + hi]
            q_r_swapped = q[:, 2 * MLA_HEADS * MLA_NOPE + lo:2 * MLA_HEADS * MLA_NOPE + hi]
            ssq = jnp.sum(q_n * q_n + q_r * q_r, axis=-1, keepdims=True)
            rstd = lax.rsqrt(ssq * (1.0 / MLA_QK) + NORM_EPS) * scale
            q_pe = q_r * q_cos + q_r_swapped * q_sin
            qa_ref[hd, rows, :] = jnp.concatenate([q_n * g_qn_ref[...] * rstd, q_pe * rstd],
                                                  axis=1).astype(BF16)
            k_n = kv[:, lo:hi]
            ssq = jnp.sum(k_n * k_n + sq_kr, axis=-1, keepdims=True)
            rstd = lax.rsqrt(ssq * (1.0 / MLA_QK) + NORM_EPS)
            ka_ref[hd, rows, :] = jnp.concatenate([k_n * g_kn_ref[...] * rstd, k_pe * rstd],
                                                  axis=1).astype(BF16)

    tables = None
    for i, t in enumerate(tiles):
        project(t)
        if i == 0:
            tables = rope_tables()
        else:
            heads(tiles[i - 1], *tables)
        up_project(t)
    heads(tiles[-1], *tables)


def _const_spec(shape):
    nd = len(shape)
    return pl.BlockSpec(shape, lambda *_: (0,) * nd, pipeline_mode=pl.Buffered(1))


def _input_stage(x2, pos, invf, g_mix, w_qig, w_f, w_mla, g_cq, w_qup, g_ckv, w_kvup,
                 g_qn, g_qr, g_kn, g_kr, tm):
    t_tok, d_model = x2.shape
    consts = (invf, g_mix, w_qig, w_f, w_mla, g_cq, w_qup, g_ckv, w_kvup, g_qn, g_qr, g_kn, g_kr)
    return pl.pallas_call(
        _input_kernel,
        grid=(t_tok // tm,),
        in_specs=[pl.BlockSpec((tm, d_model), lambda i: (i, 0)),
                  pl.BlockSpec((1, tm), lambda i: (0, i))] + [_const_spec(c.shape) for c in consts],
        out_specs=[pl.BlockSpec((3 * HG_HEADS, tm, HG_D), lambda i: (0, i, 0)),
                   pl.BlockSpec((2 * HG_HEADS, tm, HG_D), lambda i: (0, i, 0)),
                   pl.BlockSpec((MLA_HEADS, tm, QK_PAD), lambda i: (0, i, 0)),
                   pl.BlockSpec((MLA_HEADS, tm, QK_PAD), lambda i: (0, i, 0)),
                   pl.BlockSpec((MLA_HEADS, tm, MLA_V), lambda i: (0, i, 0))],
        out_shape=[jax.ShapeDtypeStruct((3 * HG_HEADS, t_tok, HG_D), BF16),
                   jax.ShapeDtypeStruct((2 * HG_HEADS, t_tok, HG_D), F32),
                   jax.ShapeDtypeStruct((MLA_HEADS, t_tok, QK_PAD), BF16),
                   jax.ShapeDtypeStruct((MLA_HEADS, t_tok, QK_PAD), BF16),
                   jax.ShapeDtypeStruct((MLA_HEADS, t_tok, MLA_V), BF16)],
        compiler_params=pltpu.CompilerParams(dimension_semantics=("arbitrary",),
                                             vmem_limit_bytes=VMEM_LIMIT),
        name="input_stage",
    )(x2, pos, *consts)


def _hg_levels(leaf):
    sizes = []
    half = HG_CHUNK // 2
    while half >= leaf:
        sizes.append(half)
        half //= 2
    return tuple(sizes)


def _hg_gate_terms(logit, lb):
    f = lb + (1.0 - lb) * _sigmoid(logit)
    g = jnp.log2(f)
    g_hi = g.astype(BF16)
    r1 = g - g_hi.astype(F32)
    g_mid = r1.astype(BF16)
    g_lo = (r1 - g_mid.astype(F32)).astype(BF16)
    return 1.0 - f, jnp.concatenate([g_hi, g_mid, g_lo], axis=0)


def _hg_block_decay(b, rev):
    n_blk = HG_CHUNK // HG_BLOCK
    edge = [b[i * HG_BLOCK:i * HG_BLOCK + 1] if rev else b[(i + 1) * HG_BLOCK - 1:(i + 1) * HG_BLOCK]
            for i in range(n_blk)]
    worst = edge[-1] if rev else edge[0]
    for i in range(1, n_blk):
        worst = jnp.minimum(worst, edge[i - 1] - edge[i] if rev else edge[i] - edge[i - 1])
    return worst


def _hg_pairs_blocked(jobs):
    c, n_blk = HG_CHUNK, HG_CHUNK // HG_BLOCK
    shape3 = (n_blk, HG_BLOCK, HG_D)
    zero_row = jnp.zeros((1, 1, HG_D), F32)
    zero_blk = jnp.zeros((HG_BLOCK, HG_D), BF16)
    rows = lambda pieces: jnp.concatenate([pieces.get(i, zero_blk) for i in range(n_blk)], axis=0)
    halves = [h // HG_BLOCK for h in _hg_levels(HG_BLOCK)]
    for j in jobs:
        rev = j["rev"]
        b3 = j["b"].reshape(shape3)
        if rev:
            end = b3[:, 0:1]
            ref = jnp.concatenate([end[1:], zero_row], axis=0)
        else:
            end = b3[:, HG_BLOCK - 1:HG_BLOCK]
            ref = jnp.concatenate([zero_row, end[:-1]], axis=0)
        order = [n_blk - 1 - i for i in range(n_blk)] if rev else list(range(n_blk))
        block_at = {o: i for i, o in enumerate(order)}
        b_tot = end[block_at[n_blk - 1]]
        q_b = j["q"].reshape(shape3) * jnp.exp2(b3 - ref)
        k_f = j["k"].reshape(shape3) * jnp.exp2(end - b3)
        q16 = [q_b[i].astype(BF16) for i in range(n_blk)]
        k16 = [k_f[i].astype(BF16) for i in range(n_blk)]
        j["dec"] = jnp.exp2(b_tot)
        j["q_in"] = rows({i: (q_b[i] * jnp.exp2(ref[i])).astype(BF16) for i in range(n_blk)})
        j["k_out"] = rows({i: (k_f[i] * jnp.exp2(b_tot - end[i])).astype(BF16) for i in range(n_blk)})
        q_x = [rows({i: q16[i]}) for i in range(n_blk)]
        k_x = [rows({i: (k_f[i] * jnp.exp2(ref[i] - end[i])).astype(BF16)}) for i in range(n_blk)]
        for half in halves:
            for first in range(0, n_blk, 2 * half):
                mid = end[block_at[first + half - 1]]
                q_l, k_l = {}, {}
                for i, o in enumerate(order):
                    if first + half <= o < first + 2 * half:
                        q_l[i] = (q16[i] if o == first + half
                                  else (q_b[i] * jnp.exp2(ref[i] - mid)).astype(BF16))
                    elif first <= o < first + half:
                        k_l[i] = (k16[i] if o == first + half - 1
                                  else (k_f[i] * jnp.exp2(mid - end[i])).astype(BF16))
                q_x.append(rows(q_l))
                k_x.append(rows(k_l))
        j["q_x"], j["k_x"] = jnp.concatenate(q_x, axis=1), jnp.concatenate(k_x, axis=1)

    pairs = list(zip(jobs[0::2], jobs[1::2]))
    for fwd, bwd in pairs:
        p = _dot_nt(jnp.concatenate([fwd["q_x"], bwd["q_x"]], axis=0),
                    jnp.concatenate([fwd["k_x"], bwd["k_x"]], axis=0))
        fwd["p"] = (p[:c, :c] * fwd["causal_mask"]).astype(BF16)
        bwd["p"] = (p[c:, c:] * bwd["causal_mask"]).astype(BF16)
    for fwd, bwd in pairs:
        st_add = _dot(fwd["v"].T.astype(BF16), jnp.concatenate([fwd["k_out"], bwd["k_out"]], axis=1))
        fwd["st_add"], bwd["st_add"] = st_add[:, :HG_D], st_add[:, HG_D:]
    for fwd, bwd in pairs:
        o = _dot(jnp.concatenate([fwd["p"], bwd["p"]], axis=0), fwd["v"].astype(BF16))
        fwd["o"], bwd["o"] = o[:c], o[c:]


def _hg_chunks(jobs, lvl_masks, blocked):
    c = HG_CHUNK
    if blocked:
        _hg_pairs_blocked(jobs)
        return [(j["o"], j["q_in"], j["st_add"], j["dec"]) for j in jobs]

    for j in jobs:
        q, k, b = j["q"], j["k"], j["b"]
        b_tot = b[0:1] if j["rev"] else b[c - 1:c]
        j["dec"] = jnp.exp2(b_tot)
        j["q_in"] = (q * jnp.exp2(b)).astype(BF16)
        k_out = (k * jnp.exp2(b_tot - b)).astype(BF16)
        j["st_add"] = _dot(j["v"].T.astype(BF16), k_out)
        j["p"] = None

    for lvl, half in enumerate(_hg_levels(HG_LEAF)):
        n_grp = c // (2 * half)
        shape3 = (n_grp, 2 * half, HG_D)
        zero = jnp.zeros((n_grp, half, HG_D), F32)
        for j in jobs:
            b3, q3, k3 = j["b"].reshape(shape3), j["q"].reshape(shape3), j["k"].reshape(shape3)
            if j["rev"]:
                mid = b3[:, half:half + 1]
                q_l = jnp.concatenate([q3[:, :half] * jnp.exp2(b3[:, :half] - mid), zero], axis=1)
                k_l = jnp.concatenate([zero, k3[:, half:] * jnp.exp2(mid - b3[:, half:])], axis=1)
            else:
                mid = b3[:, half - 1:half]
                q_l = jnp.concatenate([zero, q3[:, half:] * jnp.exp2(b3[:, half:] - mid)], axis=1)
                k_l = jnp.concatenate([k3[:, :half] * jnp.exp2(mid - b3[:, :half]), zero], axis=1)
            p_l = _dot_nt(q_l.reshape(c, HG_D).astype(BF16), k_l.reshape(c, HG_D).astype(BF16))
            p_l = p_l * lvl_masks[lvl] if n_grp > 1 else p_l
            j["p"] = p_l if j["p"] is None else j["p"] + p_l

    for j in jobs:
        j["kpad"][HG_LEAF:HG_LEAF + c, :] = j["k"]
        j["bpad"][HG_LEAF:HG_LEAF + c, :] = j["b"]
    for j in jobs:
        w = [(j["q"] * j["k"]).astype(BF16)]
        for d in range(1, HG_LEAF):
            lo = HG_LEAF + d if j["rev"] else HG_LEAF - d
            k_s = j["kpad"][lo:lo + c, :]
            b_s = j["bpad"][lo:lo + c, :]
            w.append((j["q"] * k_s * jnp.exp2(j["b"] - b_s)).astype(BF16))
        j["row_sums"] = _dot(jnp.concatenate(w, axis=1), j["spread"])
    for j in jobs:
        j["p_leaf"] = (pltpu.roll(j["row_sums"], 0, 1, stride=1, stride_axis=0)
                       * j["leaf_mask"])

    outs = []
    for j in jobs:
        p = (j["p"] + j["p_leaf"]).astype(BF16)
        outs.append((_dot(p, j["v"].astype(BF16)), j["q_in"], j["st_add"], j["dec"]))
    return outs


def _hgrn_kernel(q_ref, i_ref, g_ref, ff_ref, fb_ref, lbp_ref, gain_ref, tri_ref, lvl_ref, leaf_ref,
                 causal_ref, spread_ref, out_ref, k_ref, b_ref, o_ref, qin_ref, add_ref, dec_ref,
                 st_ref, kpad_ref, bpad_ref):
    s_len = q_ref.shape[0]
    n_chunk = s_len // HG_CHUNK
    logit_refs = (ff_ref, fb_ref)
    chunk_rows = lambda ci: pl.ds(pl.multiple_of(ci * HG_CHUNK, HG_CHUNK), HG_CHUNK)

    def lower_bound(d):
        p = lbp_ref[d]
        e = jnp.exp(p - jnp.max(p, axis=0, keepdims=True))
        return e[0:1] / jnp.sum(e, axis=0, keepdims=True)

    lbs = (lower_bound(0), lower_bound(1))

    def gates(n, worst):
        rows = [chunk_rows(n * HG_UNROLL + u) for u in range(HG_UNROLL)]
        for d in range(2):
            terms = []
            for r in rows:
                k, g3 = _hg_gate_terms(logit_refs[d][r, :], lbs[d])
                k_ref[d, r, :] = k
                terms.append(g3)
            b_all = _dot(tri_ref[d], jnp.concatenate(terms, axis=1))
            for u, r in enumerate(rows):
                b = b_all[:, u * HG_D:(u + 1) * HG_D]
                b_ref[d, r, :] = b
                worst = jnp.minimum(worst, _hg_block_decay(b, bool(d)))
        return worst

    worst = lax.fori_loop(0, n_chunk // HG_UNROLL, gates, jnp.zeros((1, HG_D), F32))
    blocks_representable = jnp.min(worst) >= -HG_MAX_BLOCK_DECAY

    def within_chunks(blocked):
        def body(n, carry):
            lvl_masks = [lvl_ref[l] for l in range(lvl_ref.shape[0])]
            jobs = []
            for u in range(HG_UNROLL):
                ci = n * HG_UNROLL + u
                rows = chunk_rows(ci)
                for d, rev in enumerate((False, True)):
                    slot = d * HG_UNROLL + u
                    jobs.append(dict(
                        ci=ci, rows=rows, d=d, rev=rev, q=q_ref[rows, :].astype(F32),
                        k=k_ref[d, rows, :], b=b_ref[d, rows, :], v=i_ref[rows, :].astype(F32),
                        kpad=kpad_ref.at[slot], bpad=bpad_ref.at[slot], leaf_mask=leaf_ref[d],
                        causal_mask=causal_ref[d], spread=spread_ref[d]))
            for j, (o, q_in, st_add, dec) in zip(jobs, _hg_chunks(jobs, lvl_masks, blocked)):
                o_ref[j["d"], j["rows"], :] = o
                qin_ref[j["rows"], j["d"] * HG_D:(j["d"] + 1) * HG_D] = q_in
                add_ref[j["d"], j["ci"]] = st_add
                dec_ref[j["d"], j["ci"]] = dec
            return carry
        return body

    @pl.when(blocks_representable)
    def _():
        lax.fori_loop(0, n_chunk // HG_UNROLL, within_chunks(True), 0)

    @pl.when(jnp.logical_not(blocks_representable))
    def _():
        kpad_ref[...] = jnp.zeros_like(kpad_ref)
        bpad_ref[...] = jnp.zeros_like(bpad_ref)
        lax.fori_loop(0, n_chunk // HG_UNROLL, within_chunks(False), 0)

    def across_chunks(n, states):
        new_states = []
        for d, st in enumerate(states):
            ci = (n_chunk - 1 - n) if d else n
            st_ref[ci, :, d * HG_D:(d + 1) * HG_D] = st.astype(BF16)
            new_states.append(dec_ref[d, ci] * st + add_ref[d, ci])
        return tuple(new_states)

    zero_state = jnp.zeros((HG_D, HG_D), F32)
    lax.fori_loop(0, n_chunk, across_chunks, (zero_state, zero_state), unroll=True)

    def finish(n, carry):
        rows = chunk_rows(n)
        o = o_ref[0, rows, :] + o_ref[1, rows, :] + _dot_nt(qin_ref[rows, :], st_ref[n])
        y = _rms(o, gain_ref[...])
        gate = g_ref[rows, :].astype(F32)
        out_ref[rows, :] = (y * gate * _sigmoid(gate)).astype(BF16)
        return carry

    lax.fori_loop(0, n_chunk, finish, 0, unroll=8)


def _hgrn_stage(qig, fgate, lb_param, gain, batch, s_len):
    assert HG_CHUNK == LANES, "the leaf path rotates each row of a (C, C) tile by its row index"
    t_tok = qig.shape[1]
    c = HG_CHUNK
    r = jnp.arange(c)
    t_i, s_i = r[:, None], r[None, :]
    lower = s_i <= t_i
    causal = jnp.stack([lower, lower.T])
    tri = jnp.tile(causal, (1, 1, 3)).astype(BF16)
    lvl_masks = jnp.stack([t_i // (2 * h) == s_i // (2 * h) for h in _hg_levels(HG_LEAF)]).astype(F32)
    leaf_mask = (causal & (t_i // HG_LEAF == s_i // HG_LEAF)).astype(F32)
    dist = jnp.repeat(jnp.arange(HG_LEAF), HG_D)[:, None]
    spread = jnp.stack([(-s_i) % HG_LEAF == dist, s_i % HG_LEAF == dist]).astype(BF16)
    blk = lambda col: pl.BlockSpec((None, s_len, HG_D),
                                   lambda b, h, col=col: (col * HG_HEADS + h, b, 0))
    consts = (tri, lvl_masks, leaf_mask, causal.astype(F32), spread)
    seq = lambda dtype: pltpu.VMEM((2, s_len, HG_D), dtype)
    return pl.pallas_call(
        _hgrn_kernel,
        grid=(batch, HG_HEADS),
        in_specs=[blk(0), blk(1), blk(2), blk(0), blk(1),
                  pl.BlockSpec((2, 2, HG_D), lambda b, h: (0, 0, h)),
                  pl.BlockSpec((1, HG_D), lambda b, h: (0, h))] + [_const_spec(t.shape) for t in consts],
        out_specs=pl.BlockSpec((None, s_len, HG_D), lambda b, h: (h, b, 0)),
        out_shape=jax.ShapeDtypeStruct((HG_HEADS, t_tok, HG_D), BF16),
        scratch_shapes=[seq(F32), seq(F32), seq(F32), pltpu.VMEM((s_len, 2 * HG_D), BF16),
                        pltpu.VMEM((2, s_len // c, HG_D, HG_D), F32),
                        pltpu.VMEM((2, s_len // c, 1, HG_D), F32),
                        pltpu.VMEM((s_len // c, HG_D, 2 * HG_D), BF16),
                        pltpu.VMEM((2 * HG_UNROLL, c + 2 * HG_LEAF, HG_D), F32),
                        pltpu.VMEM((2 * HG_UNROLL, c + 2 * HG_LEAF, HG_D), F32)],
        compiler_params=pltpu.CompilerParams(dimension_semantics=("arbitrary", "arbitrary"),
                                             vmem_limit_bytes=VMEM_LIMIT),
        name="hgrn_stage",
    )(qig, qig, qig, fgate, fgate, lb_param, gain, *consts)


def _attn_kernel(q_ref, k_ref, v_ref, o_ref):
    n_heads, s_len, _ = q_ref.shape
    sub = min(ATTN_SUB, s_len)
    items = [(h, slice(i * sub, (i + 1) * sub)) for h in range(n_heads) for i in range(s_len // sub)]
    scores = lambda h, rows: _dot_nt(q_ref[h, rows, :], k_ref[h])
    pending = [scores(*item) for item in items[:ATTN_DEPTH]]
    for n, (h, rows) in enumerate(items):
        s = pending.pop(0)
        if n + ATTN_DEPTH < len(items):
            pending.append(scores(*items[n + ATTN_DEPTH]))
        p = jnp.exp2(s - jnp.max(s, axis=-1, keepdims=True))
        o = _dot(p.astype(BF16), v_ref[h])
        o_ref[h, rows, :] = (o * (1.0 / jnp.sum(p, axis=-1, keepdims=True))).astype(BF16)


def _attn_stage(q_all, k_all, v, batch, s_len):
    t_tok = v.shape[1]
    n_grp = MLA_HEADS // ATTN_HEADS
    return pl.pallas_call(
        _attn_kernel,
        grid=(batch, n_grp),
        in_specs=[pl.BlockSpec((ATTN_HEADS, s_len, QK_PAD), lambda b, g: (g, b, 0)),
                  pl.BlockSpec((ATTN_HEADS, s_len, QK_PAD), lambda b, g: (g, b, 0)),
                  pl.BlockSpec((ATTN_HEADS, s_len, MLA_V), lambda b, g: (g, b, 0))],
        out_specs=pl.BlockSpec((ATTN_HEADS, s_len, MLA_V), lambda b, g: (g, b, 0)),
        out_shape=jax.ShapeDtypeStruct((MLA_HEADS, t_tok, MLA_V), BF16),
        compiler_params=pltpu.CompilerParams(dimension_semantics=("arbitrary",) * 2,
                                             vmem_limit_bytes=VMEM_LIMIT),
        name="attn_stage",
    )(q_all, k_all, v)


def _output_kernel(x_ref, a_ref, ob_ref, g_mla_ref, w_oa_ref, w_ob_ref, g_ffn_ref, w_up_ref,
                   w_dn_ref, out_ref):
    ob = _rms(_load_slabs(ob_ref).astype(F32), g_mla_ref[...]).astype(BF16)
    x1 = x_ref[...] + _dot(_load_slabs(a_ref), w_oa_ref[...]) + _dot(ob, w_ob_ref[...])
    h = _rms(x1, g_ffn_ref[...]).astype(BF16)
    up = jnp.maximum(_dot(h, w_up_ref[...]), 0.0)
    out_ref[...] = x1 + _dot((up * up).astype(BF16), w_dn_ref[...])


def _output_stage(x2, a, ob, g_mla, w_oa, w_ob, g_ffn, w_up, w_dn, tm):
    t_tok, d_model = x2.shape
    consts = (g_mla, w_oa, w_ob, g_ffn, w_up, w_dn)
    return pl.pallas_call(
        _output_kernel,
        grid=(t_tok // tm,),
        in_specs=[pl.BlockSpec((tm, d_model), lambda i: (i, 0)),
                  pl.BlockSpec((HG_HEADS, tm, HG_D), lambda i: (0, i, 0)),
                  pl.BlockSpec((MLA_HEADS, tm, MLA_V), lambda i: (0, i, 0))]
                 + [_const_spec(c.shape) for c in consts],
        out_specs=pl.BlockSpec((tm, d_model), lambda i: (i, 0)),
        out_shape=jax.ShapeDtypeStruct((t_tok, d_model), F32),
        compiler_params=pltpu.CompilerParams(dimension_semantics=("arbitrary",),
                                             vmem_limit_bytes=VMEM_LIMIT),
        name="output_stage",
    )(x2, a, ob, *consts)


def kernel(x, positions, g_mix_norm, w_in, lb_param, g_hgrn_out, g_cq, w_q_up, g_ckv, w_kv_up,
           g_q_norm, g_k_norm, g_mla_out, w_out, g_ffn_norm, w_up, w_down):
    batch, s_len, d_model = x.shape
    assert g_mix_norm.shape[0] == 1 and lb_param.shape[1] == 2, "one layer"
    assert s_len % (HG_CHUNK * HG_UNROLL) == 0
    t_tok = batch * s_len
    row = lambda t: t.reshape(1, -1).astype(F32)

    w = w_in[0]
    o_q, o_ff, o_fb, o_i, o_g, o_cq = (HG_W * n for n in range(6))
    o_ckv = o_cq + Q_LORA
    o_kr = o_ckv + KV_LORA
    w_qig = jnp.concatenate([w[:, o_q:o_ff], w[:, o_i:o_g], w[:, o_g:o_cq]], axis=1).astype(BF16)
    w_f = w[:, o_ff:o_i].astype(BF16)
    half = MLA_ROPE // 2
    swap = lambda t: jnp.concatenate([t[..., half:], t[..., :half]], axis=-1)
    pad_lanes = lambda t: jnp.pad(t, [(0, 0)] * (t.ndim - 1) + [(0, LANES - MLA_ROPE)])
    w_kr = w[:, o_kr:]
    w_mla = jnp.concatenate([w[:, o_cq:o_kr], pad_lanes(w_kr), pad_lanes(swap(w_kr))],
                            axis=1).astype(BF16)

    wq = w_q_up[0].reshape(Q_LORA, MLA_HEADS, MLA_QK)
    wq_rope = wq[:, :, MLA_NOPE:]
    w_qup = jnp.concatenate([wq[:, :, :MLA_NOPE].reshape(Q_LORA, -1),
                             pad_lanes(wq_rope).reshape(Q_LORA, -1),
                             pad_lanes(swap(wq_rope)).reshape(Q_LORA, -1)], axis=1).astype(BF16)
    wkv = w_kv_up[0].reshape(KV_LORA, MLA_HEADS, MLA_NOPE + MLA_V)
    w_kvup = jnp.concatenate([wkv[:, :, :MLA_NOPE].reshape(KV_LORA, -1),
                              wkv[:, :, MLA_NOPE:].reshape(KV_LORA, -1)], axis=1).astype(BF16)
    rope_gain = lambda t: jnp.stack([pad_lanes(t), pad_lanes(swap(t))]).astype(F32)
    g_qn, g_qr = row(g_q_norm[0, :MLA_NOPE]), rope_gain(g_q_norm[0, MLA_NOPE:])
    g_kn, g_kr = row(g_k_norm[0, :MLA_NOPE]), rope_gain(g_k_norm[0, MLA_NOPE:])
    invf = (ROPE_THETA ** (-jnp.arange(0, MLA_ROPE, 2, dtype=F32) / MLA_ROPE)).reshape(-1, 1)

    x2 = x.reshape(t_tok, d_model)
    pos = positions.reshape(1, t_tok).astype(F32)
    tm = min(512, s_len)
    qig, fgate, q_all, k_all, v = _input_stage(
        x2, pos, invf, row(g_mix_norm[0]), w_qig, w_f, w_mla, row(g_cq[0]), w_qup, row(g_ckv[0]),
        w_kvup, g_qn, g_qr, g_kn, g_kr, min(INPUT_TILE, s_len))

    a = _hgrn_stage(qig, fgate, lb_param.astype(F32), g_hgrn_out[0].reshape(1, HG_W).astype(F32),
                    batch, s_len)
    ob = _attn_stage(q_all, k_all, v, batch, s_len)

    wo = w_out[0].astype(BF16)
    out = _output_stage(x2, a, ob, row(g_mla_out[0]), wo[:HG_W], wo[HG_W:], row(g_ffn_norm[0]),
                        w_up[0].astype(BF16), w_down[0].astype(BF16), tm)
    return out.reshape(batch, s_len, d_model)
```

```python
import functools

import jax
import jax.numpy as jnp
from jax import lax
from jax.experimental import pallas as pl
from jax.experimental.pallas import tpu as pltpu

NORM_EPS = 1e-6
HG_HEADS = 4
HG_D = 128
HG_W = HG_HEADS * HG_D
MLA_HEADS = 4
MLA_NOPE = 128
MLA_ROPE = 64
MLA_V = 128
MLA_QK = MLA_NOPE + MLA_ROPE
Q_LORA = 384
KV_LORA = 256
ROPE_THETA = 10000.0
LANES = 128
QK_PAD = 2 * LANES

HG_CHUNK = 128
HG_LEAF = 8
HG_BLOCK = 64
HG_MAX_BLOCK_DECAY = 100.0
HG_UNROLL = 8
ATTN_SUB = 256
ATTN_HEADS = 2
ATTN_DEPTH = 1
INPUT_TILE = 512
INPUT_SUB = 256
LOG2_E = 1.4426950408889634
VMEM_LIMIT = 56 * 1024 * 1024

F32 = jnp.float32
BF16 = jnp.bfloat16


def _dot(a, b):
    return jnp.dot(a, b, preferred_element_type=F32)


def _dot_nt(a, b):
    return lax.dot_general(a, b, (((1,), (1,)), ((), ())), preferred_element_type=F32)


def _rms(t, gain):
    return t * lax.rsqrt(jnp.mean(t * t, axis=-1, keepdims=True) + NORM_EPS) * gain


def _sigmoid(t):
    return 1.0 / (1.0 + jnp.exp2(t * -LOG2_E))


def _store_slabs(ref, rows, t):
    for j in range(ref.shape[0]):
        ref[j, rows, :] = t[:, j * LANES:(j + 1) * LANES].astype(ref.dtype)


def _load_slabs(ref):
    return jnp.concatenate([ref[j] for j in range(ref.shape[0])], axis=1)


def _input_kernel(x_ref, pos_ref, invf_ref, g_mix_ref, w_qig_ref, w_f_ref, w_mla_ref, g_cq_ref,
                  w_qup_ref, g_ckv_ref, w_kvup_ref, g_qn_ref, g_qr_ref, g_kn_ref, g_kr_ref,
                  qig_ref, f_ref, qa_ref, ka_ref, v_ref):
    tm = x_ref.shape[0]
    sub = min(INPUT_SUB, tm)
    tiles = [dict(rows=slice(i * sub, (i + 1) * sub)) for i in range(tm // sub)]
    scale = MLA_QK ** -0.5 * LOG2_E

    def project(t):
        h = _rms(x_ref[t["rows"], :], g_mix_ref[...]).astype(BF16)
        _store_slabs(qig_ref, t["rows"], _dot(h, w_qig_ref[...]))
        _store_slabs(f_ref, t["rows"], _dot(h, w_f_ref[...]))
        t["m"] = _dot(h, w_mla_ref[...])

    def up_project(t):
        c_q = t["m"][:, :Q_LORA]
        c_kv = t["m"][:, Q_LORA:Q_LORA + KV_LORA]
        t["q"] = _dot(_rms(c_q, g_cq_ref[...]).astype(BF16), w_qup_ref[...])
        t["kv"] = _dot(_rms(c_kv, g_ckv_ref[...]).astype(BF16), w_kvup_ref[...])

    def rope_tables():
        ang = invf_ref[...] * pos_ref[...]
        c, s = jnp.cos(ang), jnp.sin(ang)
        z = jnp.zeros_like(c)
        return jnp.concatenate([c, c, z, z], axis=0).T, jnp.concatenate([-s, s, z, z], axis=0).T

    def heads(t, cos_t, sin_t):
        rows, q, kv = t["rows"], t["q"], t["kv"]
        _store_slabs(v_ref, rows, kv[:, MLA_HEADS * MLA_NOPE:])
        cos_r, sin_r = cos_t[rows, :], sin_t[rows, :]
        q_cos, q_sin = g_qr_ref[0:1] * cos_r, g_qr_ref[1:2] * sin_r
        rope_cols = Q_LORA + KV_LORA
        k_r = t["m"][:, rope_cols:rope_cols + LANES]
        k_pe = (k_r * (g_kr_ref[0:1] * cos_r)
                + t["m"][:, rope_cols + LANES:] * (g_kr_ref[1:2] * sin_r))
        sq_kr = k_r * k_r
        for hd in range(MLA_HEADS):
            lo, hi = hd * LANES, (hd + 1) * LANES
            q_n = q[:, lo:hi]
            q_r = q[:, MLA_HEADS * MLA_NOPE + lo:MLA_HEADS * MLA_NOPE + hi]
            q_r_swapped = q[:, 2 * MLA_HEADS * MLA_NOPE + lo:2 * MLA_HEADS * MLA_NOPE + hi]
            ssq = jnp.sum(q_n * q_n + q_r * q_r, axis=-1, keepdims=True)
            rstd = lax.rsqrt(ssq * (1.0 / MLA_QK) + NORM_EPS) * scale
            q_pe = q_r * q_cos + q_r_swapped * q_sin
            qa_ref[hd, rows, :] = jnp.concatenate([q_n * g_qn_ref[...] * rstd, q_pe * rstd],
                                                  axis=1).astype(BF16)
            k_n = kv[:, lo:hi]
            ssq = jnp.sum(k_n * k_n + sq_kr, axis=-1, keepdims=True)
            rstd = lax.rsqrt(ssq * (1.0 / MLA_QK) + NORM_EPS)
            ka_ref[hd, rows, :] = jnp.concatenate([k_n * g_kn_ref[...] * rstd, k_pe * rstd],
                                                  axis=1).astype(BF16)

    tables = None
    for i, t in enumerate(tiles):
        project(t)
        if i == 0:
            tables = rope_tables()
        else:
            heads(tiles[i - 1], *tables)
        up_project(t)
    heads(tiles[-1], *tables)


def _const_spec(shape):
    nd = len(shape)
    return pl.BlockSpec(shape, lambda *_: (0,) * nd, pipeline_mode=pl.Buffered(1))


def _input_stage(x2, pos, invf, g_mix, w_qig, w_f, w_mla, g_cq, w_qup, g_ckv, w_kvup,
                 g_qn, g_qr, g_kn, g_kr, tm):
    t_tok, d_model = x2.shape
    consts = (invf, g_mix, w_qig, w_f, w_mla, g_cq, w_qup, g_ckv, w_kvup, g_qn, g_qr, g_kn, g_kr)
    return pl.pallas_call(
        _input_kernel,
        grid=(t_tok // tm,),
        in_specs=[pl.BlockSpec((tm, d_model), lambda i: (i, 0)),
                  pl.BlockSpec((1, tm), lambda i: (0, i))] + [_const_spec(c.shape) for c in consts],
        out_specs=[pl.BlockSpec((3 * HG_HEADS, tm, HG_D), lambda i: (0, i, 0)),
                   pl.BlockSpec((2 * HG_HEADS, tm, HG_D), lambda i: (0, i, 0)),
                   pl.BlockSpec((MLA_HEADS, tm, QK_PAD), lambda i: (0, i, 0)),
                   pl.BlockSpec((MLA_HEADS, tm, QK_PAD), lambda i: (0, i, 0)),
                   pl.BlockSpec((MLA_HEADS, tm, MLA_V), lambda i: (0, i, 0))],
        out_shape=[jax.ShapeDtypeStruct((3 * HG_HEADS, t_tok, HG_D), BF16),
                   jax.ShapeDtypeStruct((2 * HG_HEADS, t_tok, HG_D), F32),
                   jax.ShapeDtypeStruct((MLA_HEADS, t_tok, QK_PAD), BF16),
                   jax.ShapeDtypeStruct((MLA_HEADS, t_tok, QK_PAD), BF16),
                   jax.ShapeDtypeStruct((MLA_HEADS, t_tok, MLA_V), BF16)],
        compiler_params=pltpu.CompilerParams(dimension_semantics=("arbitrary",),
                                             vmem_limit_bytes=VMEM_LIMIT),
        name="input_stage",
    )(x2, pos, *consts)


def _hg_levels(leaf):
    sizes = []
    half = HG_CHUNK // 2
    while half >= leaf:
        sizes.append(half)
        half //= 2
    return tuple(sizes)


def _hg_gate_terms(logit, lb):
    f = lb + (1.0 - lb) * _sigmoid(logit)
    g = jnp.log2(f)
    g_hi = g.astype(BF16)
    r1 = g - g_hi.astype(F32)
    g_mid = r1.astype(BF16)
    g_lo = (r1 - g_mid.astype(F32)).astype(BF16)
    return 1.0 - f, jnp.concatenate([g_hi, g_mid, g_lo], axis=0)


def _hg_block_decay(b, rev):
    n_blk = HG_CHUNK // HG_BLOCK
    edge = [b[i * HG_BLOCK:i * HG_BLOCK + 1] if rev else b[(i + 1) * HG_BLOCK - 1:(i + 1) * HG_BLOCK]
            for i in range(n_blk)]
    worst = edge[-1] if rev else edge[0]
    for i in range(1, n_blk):
        worst = jnp.minimum(worst, edge[i - 1] - edge[i] if rev else edge[i] - edge[i - 1])
    return worst


def _hg_pairs_blocked(jobs):
    c, n_blk = HG_CHUNK, HG_CHUNK // HG_BLOCK
    shape3 = (n_blk, HG_BLOCK, HG_D)
    zero_row = jnp.zeros((1, 1, HG_D), F32)
    zero_blk = jnp.zeros((HG_BLOCK, HG_D), BF16)
    rows = lambda pieces: jnp.concatenate([pieces.get(i, zero_blk) for i in range(n_blk)], axis=0)
    halves = [h // HG_BLOCK for h in _hg_levels(HG_BLOCK)]
    for j in jobs:
        rev = j["rev"]
        b3 = j["b"].reshape(shape3)
        if rev:
            end = b3[:, 0:1]
            ref = jnp.concatenate([end[1:], zero_row], axis=0)
        else:
            end = b3[:, HG_BLOCK - 1:HG_BLOCK]
            ref = jnp.concatenate([zero_row, end[:-1]], axis=0)
        order = [n_blk - 1 - i for i in range(n_blk)] if rev else list(range(n_blk))
        block_at = {o: i for i, o in enumerate(order)}
        b_tot = end[block_at[n_blk - 1]]
        q_b = j["q"].reshape(shape3) * jnp.exp2(b3 - ref)
        k_f = j["k"].reshape(shape3) * jnp.exp2(end - b3)
        q16 = [q_b[i].astype(BF16) for i in range(n_blk)]
        k16 = [k_f[i].astype(BF16) for i in range(n_blk)]
        j["dec"] = jnp.exp2(b_tot)
        j["q_in"] = rows({i: (q_b[i] * jnp.exp2(ref[i])).astype(BF16) for i in range(n_blk)})
        j["k_out"] = rows({i: (k_f[i] * jnp.exp2(b_tot - end[i])).astype(BF16) for i in range(n_blk)})
        q_x = [rows({i: q16[i]}) for i in range(n_blk)]
        k_x = [rows({i: (k_f[i] * jnp.exp2(ref[i] - end[i])).astype(BF16)}) for i in range(n_blk)]
        for half in halves:
            for first in range(0, n_blk, 2 * half):
                mid = end[block_at[first + half - 1]]
                q_l, k_l = {}, {}
                for i, o in enumerate(order):
                    if first + half <= o < first + 2 * half:
                        q_l[i] = (q16[i] if o == first + half
                                  else (q_b[i] * jnp.exp2(ref[i] - mid)).astype(BF16))
                    elif first <= o < first + half:
                        k_l[i] = (k16[i] if o == first + half - 1
                                  else (k_f[i] * jnp.exp2(mid - end[i])).astype(BF16))
                q_x.append(rows(q_l))
                k_x.append(rows(k_l))
        j["q_x"], j["k_x"] = jnp.concatenate(q_x, axis=1), jnp.concatenate(k_x, axis=1)

    pairs = list(zip(jobs[0::2], jobs[1::2]))
    for fwd, bwd in pairs:
        p = _dot_nt(jnp.concatenate([fwd["q_x"], bwd["q_x"]], axis=0),
                    jnp.concatenate([fwd["k_x"], bwd["k_x"]], axis=0))
        fwd["p"] = (p[:c, :c] * fwd["causal_mask"]).astype(BF16)
        bwd["p"] = (p[c:, c:] * bwd["causal_mask"]).astype(BF16)
    for fwd, bwd in pairs:
        st_add = _dot(fwd["v"].T.astype(BF16), jnp.concatenate([fwd["k_out"], bwd["k_out"]], axis=1))
        fwd["st_add"], bwd["st_add"] = st_add[:, :HG_D], st_add[:, HG_D:]
    for fwd, bwd in pairs:
        o = _dot(jnp.concatenate([fwd["p"], bwd["p"]], axis=0), fwd["v"].astype(BF16))
        fwd["o"], bwd["o"] = o[:c], o[c:]


def _hg_chunks(jobs, lvl_masks, blocked):
    c = HG_CHUNK
    if blocked:
        _hg_pairs_blocked(jobs)
        return [(j["o"], j["q_in"], j["st_add"], j["dec"]) for j in jobs]

    for j in jobs:
        q, k, b = j["q"], j["k"], j["b"]
        b_tot = b[0:1] if j["rev"] else b[c - 1:c]
        j["dec"] = jnp.exp2(b_tot)
        j["q_in"] = (q * jnp.exp2(b)).astype(BF16)
        k_out = (k * jnp.exp2(b_tot - b)).astype(BF16)
        j["st_add"] = _dot(j["v"].T.astype(BF16), k_out)
        j["p"] = None

    for lvl, half in enumerate(_hg_levels(HG_LEAF)):
        n_grp = c // (2 * half)
        shape3 = (n_grp, 2 * half, HG_D)
        zero = jnp.zeros((n_grp, half, HG_D), F32)
        for j in jobs:
            b3, q3, k3 = j["b"].reshape(shape3), j["q"].reshape(shape3), j["k"].reshape(shape3)
            if j["rev"]:
                mid = b3[:, half:half + 1]
                q_l = jnp.concatenate([q3[:, :half] * jnp.exp2(b3[:, :half] - mid), zero], axis=1)
                k_l = jnp.concatenate([zero, k3[:, half:] * jnp.exp2(mid - b3[:, half:])], axis=1)
            else:
                mid = b3[:, half - 1:half]
                q_l = jnp.concatenate([zero, q3[:, half:] * jnp.exp2(b3[:, half:] - mid)], axis=1)
                k_l = jnp.concatenate([k3[:, :half] * jnp.exp2(mid - b3[:, :half]), zero], axis=1)
            p_l = _dot_nt(q_l.reshape(c, HG_D).astype(BF16), k_l.reshape(c, HG_D).astype(BF16))
            p_l = p_l * lvl_masks[lvl] if n_grp > 1 else p_l
            j["p"] = p_l if j["p"] is None else j["p"] + p_l

    for j in jobs:
        j["kpad"][HG_LEAF:HG_LEAF + c, :] = j["k"]
        j["bpad"][HG_LEAF:HG_LEAF + c, :] = j["b"]
    for j in jobs:
        w = [(j["q"] * j["k"]).astype(BF16)]
        for d in range(1, HG_LEAF):
            lo = HG_LEAF + d if j["rev"] else HG_LEAF - d
            k_s = j["kpad"][lo:lo + c, :]
            b_s = j["bpad"][lo:lo + c, :]
            w.append((j["q"] * k_s * jnp.exp2(j["b"] - b_s)).astype(BF16))
        j["row_sums"] = _dot(jnp.concatenate(w, axis=1), j["spread"])
    for j in jobs:
        j["p_leaf"] = (pltpu.roll(j["row_sums"], 0, 1, stride=1, stride_axis=0)
                       * j["leaf_mask"])

    outs = []
    for j in jobs:
        p = (j["p"] + j["p_leaf"]).astype(BF16)
        outs.append((_dot(p, j["v"].astype(BF16)), j["q_in"], j["st_add"], j["dec"]))
    return outs


def _hgrn_kernel(q_ref, i_ref, g_ref, ff_ref, fb_ref, lbp_ref, gain_ref, tri_ref, lvl_ref, leaf_ref,
                 causal_ref, spread_ref, out_ref, k_ref, b_ref, o_ref, qin_ref, add_ref, dec_ref,
                 st_ref, kpad_ref, bpad_ref):
    s_len = q_ref.shape[0]
    n_chunk = s_len // HG_CHUNK
    logit_refs = (ff_ref, fb_ref)
    chunk_rows = lambda ci: pl.ds(pl.multiple_of(ci * HG_CHUNK, HG_CHUNK), HG_CHUNK)

    def lower_bound(d):
        p = lbp_ref[d]
        e = jnp.exp(p - jnp.max(p, axis=0, keepdims=True))
        return e[0:1] / jnp.sum(e, axis=0, keepdims=True)

    lbs = (lower_bound(0), lower_bound(1))

    def gates(n, worst):
        rows = [chunk_rows(n * HG_UNROLL + u) for u in range(HG_UNROLL)]
        for d in range(2):
            terms = []
            for r in rows:
                k, g3 = _hg_gate_terms(logit_refs[d][r, :], lbs[d])
                k_ref[d, r, :] = k
                terms.append(g3)
            b_all = _dot(tri_ref[d], jnp.concatenate(terms, axis=1))
            for u, r in enumerate(rows):
                b = b_all[:, u * HG_D:(u + 1) * HG_D]
                b_ref[d, r, :] = b
                worst = jnp.minimum(worst, _hg_block_decay(b, bool(d)))
        return worst

    worst = lax.fori_loop(0, n_chunk // HG_UNROLL, gates, jnp.zeros((1, HG_D), F32))
    blocks_representable = jnp.min(worst) >= -HG_MAX_BLOCK_DECAY

    def within_chunks(blocked):
        def body(n, carry):
            lvl_masks = [lvl_ref[l] for l in range(lvl_ref.shape[0])]
            jobs = []
            for u in range(HG_UNROLL):
                ci = n * HG_UNROLL + u
                rows = chunk_rows(ci)
                for d, rev in enumerate((False, True)):
                    slot = d * HG_UNROLL + u
                    jobs.append(dict(
                        ci=ci, rows=rows, d=d, rev=rev, q=q_ref[rows, :].astype(F32),
                        k=k_ref[d, rows, :], b=b_ref[d, rows, :], v=i_ref[rows, :].astype(F32),
                        kpad=kpad_ref.at[slot], bpad=bpad_ref.at[slot], leaf_mask=leaf_ref[d],
                        causal_mask=causal_ref[d], spread=spread_ref[d]))
            for j, (o, q_in, st_add, dec) in zip(jobs, _hg_chunks(jobs, lvl_masks, blocked)):
                o_ref[j["d"], j["rows"], :] = o
                qin_ref[j["rows"], j["d"] * HG_D:(j["d"] + 1) * HG_D] = q_in
                add_ref[j["d"], j["ci"]] = st_add
                dec_ref[j["d"], j["ci"]] = dec
            return carry
        return body

    @pl.when(blocks_representable)
    def _():
        lax.fori_loop(0, n_chunk // HG_UNROLL, within_chunks(True), 0)

    @pl.when(jnp.logical_not(blocks_representable))
    def _():
        kpad_ref[...] = jnp.zeros_like(kpad_ref)
        bpad_ref[...] = jnp.zeros_like(bpad_ref)
        lax.fori_loop(0, n_chunk // HG_UNROLL, within_chunks(False), 0)

    def across_chunks(n, states):
        new_states = []
        for d, st in enumerate(states):
            ci = (n_chunk - 1 - n) if d else n
            st_ref[ci, :, d * HG_D:(d + 1) * HG_D] = st.astype(BF16)
            new_states.append(dec_ref[d, ci] * st + add_ref[d, ci])
        return tuple(new_states)

    zero_state = jnp.zeros((HG_D, HG_D), F32)
    lax.fori_loop(0, n_chunk, across_chunks, (zero_state, zero_state), unroll=True)

    def finish(n, carry):
        rows = chunk_rows(n)
        o = o_ref[0, rows, :] + o_ref[1, rows, :] + _dot_nt(qin_ref[rows, :], st_ref[n])
        y = _rms(o, gain_ref[...])
        gate = g_ref[rows, :].astype(F32)
        out_ref[rows, :] = (y * gate * _sigmoid(gate)).astype(BF16)
        return carry

    lax.fori_loop(0, n_chunk, finish, 0, unroll=8)


def _hgrn_stage(qig, fgate, lb_param, gain, batch, s_len):
    assert HG_CHUNK == LANES, "the leaf path rotates each row of a (C, C) tile by its row index"
    t_tok = qig.shape[1]
    c = HG_CHUNK
    r = jnp.arange(c)
    t_i, s_i = r[:, None], r[None, :]
    lower = s_i <= t_i
    causal = jnp.stack([lower, lower.T])
    tri = jnp.tile(causal, (1, 1, 3)).astype(BF16)
    lvl_masks = jnp.stack([t_i // (2 * h) == s_i // (2 * h) for h in _hg_levels(HG_LEAF)]).astype(F32)
    leaf_mask = (causal & (t_i // HG_LEAF == s_i // HG_LEAF)).astype(F32)
    dist = jnp.repeat(jnp.arange(HG_LEAF), HG_D)[:, None]
    spread = jnp.stack([(-s_i) % HG_LEAF == dist, s_i % HG_LEAF == dist]).astype(BF16)
    blk = lambda col: pl.BlockSpec((None, s_len, HG_D),
                                   lambda b, h, col=col: (col * HG_HEADS + h, b, 0))
    consts = (tri, lvl_masks, leaf_mask, causal.astype(F32), spread)
    seq = lambda dtype: pltpu.VMEM((2, s_len, HG_D), dtype)
    return pl.pallas_call(
        _hgrn_kernel,
        grid=(batch, HG_HEADS),
        in_specs=[blk(0), blk(1), blk(2), blk(0), blk(1),
                  pl.BlockSpec((2, 2, HG_D), lambda b, h: (0, 0, h)),
                  pl.BlockSpec((1, HG_D), lambda b, h: (0, h))] + [_const_spec(t.shape) for t in consts],
        out_specs=pl.BlockSpec((None, s_len, HG_D), lambda b, h: (h, b, 0)),
        out_shape=jax.ShapeDtypeStruct((HG_HEADS, t_tok, HG_D), BF16),
        scratch_shapes=[seq(F32), seq(F32), seq(F32), pltpu.VMEM((s_len, 2 * HG_D), BF16),
                        pltpu.VMEM((2, s_len // c, HG_D, HG_D), F32),
                        pltpu.VMEM((2, s_len // c, 1, HG_D), F32),
                        pltpu.VMEM((s_len // c, HG_D, 2 * HG_D), BF16),
                        pltpu.VMEM((2 * HG_UNROLL, c + 2 * HG_LEAF, HG_D), F32),
                        pltpu.VMEM((2 * HG_UNROLL, c + 2 * HG_LEAF, HG_D), F32)],
        compiler_params=pltpu.CompilerParams(dimension_semantics=("arbitrary", "arbitrary"),
                                             vmem_limit_bytes=VMEM_LIMIT),
        name="hgrn_stage",
    )(qig, qig, qig, fgate, fgate, lb_param, gain, *consts)


def _attn_kernel(q_ref, k_ref, v_ref, o_ref):
    n_heads, s_len, _ = q_ref.shape
    sub = min(ATTN_SUB, s_len)
    items = [(h, slice(i * sub, (i + 1) * sub)) for h in range(n_heads) for i in range(s_len // sub)]
    scores = lambda h, rows: _dot_nt(q_ref[h, rows, :], k_ref[h])
    pending = [scores(*item) for item in items[:ATTN_DEPTH]]
    for n, (h, rows) in enumerate(items):
        s = pending.pop(0)
        if n + ATTN_DEPTH < len(items):
            pending.append(scores(*items[n + ATTN_DEPTH]))
        p = jnp.exp2(s - jnp.max(s, axis=-1, keepdims=True))
        o = _dot(p.astype(BF16), v_ref[h])
        o_ref[h, rows, :] = (o * (1.0 / jnp.sum(p, axis=-1, keepdims=True))).astype(BF16)


def _attn_stage(q_all, k_all, v, batch, s_len):
    t_tok = v.shape[1]
    n_grp = MLA_HEADS // ATTN_HEADS
    return pl.pallas_call(
        _attn_kernel,
        grid=(batch, n_grp),
        in_specs=[pl.BlockSpec((ATTN_HEADS, s_len, QK_PAD), lambda b, g: (g, b, 0)),
                  pl.BlockSpec((ATTN_HEADS, s_len, QK_PAD), lambda b, g: (g, b, 0)),
                  pl.BlockSpec((ATTN_HEADS, s_len, MLA_V), lambda b, g: (g, b, 0))],
        out_specs=pl.BlockSpec((ATTN_HEADS, s_len, MLA_V), lambda b, g: (g, b, 0)),
        out_shape=jax.ShapeDtypeStruct((MLA_HEADS, t_tok, MLA_V), BF16),
        compiler_params=pltpu.CompilerParams(dimension_semantics=("arbitrary",) * 2,
                                             vmem_limit_bytes=VMEM_LIMIT),
        name="attn_stage",
    )(q_all, k_all, v)


def _output_kernel(x_ref, a_ref, ob_ref, g_mla_ref, w_oa_ref, w_ob_ref, g_ffn_ref, w_up_ref,
                   w_dn_ref, out_ref):
    ob = _rms(_load_slabs(ob_ref).astype(F32), g_mla_ref[...]).astype(BF16)
    x1 = x_ref[...] + _dot(_load_slabs(a_ref), w_oa_ref[...]) + _dot(ob, w_ob_ref[...])
    h = _rms(x1, g_ffn_ref[...]).astype(BF16)
    up = jnp.maximum(_dot(h, w_up_ref[...]), 0.0)
    out_ref[...] = x1 + _dot((up * up).astype(BF16), w_dn_ref[...])


def _output_stage(x2, a, ob, g_mla, w_oa, w_ob, g_ffn, w_up, w_dn, tm):
    t_tok, d_model = x2.shape
    consts = (g_mla, w_oa, w_ob, g_ffn, w_up, w_dn)
    return pl.pallas_call(
        _output_kernel,
        grid=(t_tok // tm,),
        in_specs=[pl.BlockSpec((tm, d_model), lambda i: (i, 0)),
                  pl.BlockSpec((HG_HEADS, tm, HG_D), lambda i: (0, i, 0)),
                  pl.BlockSpec((MLA_HEADS, tm, MLA_V), lambda i: (0, i, 0))]
                 + [_const_spec(c.shape) for c in consts],
        out_specs=pl.BlockSpec((tm, d_model), lambda i: (i, 0)),
        out_shape=jax.ShapeDtypeStruct((t_tok, d_model), F32),
        compiler_params=pltpu.CompilerParams(dimension_semantics=("arbitrary",),
                                             vmem_limit_bytes=VMEM_LIMIT),
        name="output_stage",
    )(x2, a, ob, *consts)


def kernel(x, positions, g_mix_norm, w_in, lb_param, g_hgrn_out, g_cq, w_q_up, g_ckv, w_kv_up,
           g_q_norm, g_k_norm, g_mla_out, w_out, g_ffn_norm, w_up, w_down):
    batch, s_len, d_model = x.shape
    assert g_mix_norm.shape[0] == 1 and lb_param.shape[1] == 2, "one layer"
    assert s_len % (HG_CHUNK * HG_UNROLL) == 0
    t_tok = batch * s_len
    row = lambda t: t.reshape(1, -1).astype(F32)

    w = w_in[0]
    o_q, o_ff, o_fb, o_i, o_g, o_cq = (HG_W * n for n in range(6))
    o_ckv = o_cq + Q_LORA
    o_kr = o_ckv + KV_LORA
    w_qig = jnp.concatenate([w[:, o_q:o_ff], w[:, o_i:o_g], w[:, o_g:o_cq]], axis=1).astype(BF16)
    w_f = w[:, o_ff:o_i].astype(BF16)
    half = MLA_ROPE // 2
    swap = lambda t: jnp.concatenate([t[..., half:], t[..., :half]], axis=-1)
    pad_lanes = lambda t: jnp.pad(t, [(0, 0)] * (t.ndim - 1) + [(0, LANES - MLA_ROPE)])
    w_kr = w[:, o_kr:]
    w_mla = jnp.concatenate([w[:, o_cq:o_kr], pad_lanes(w_kr), pad_lanes(swap(w_kr))],
                            axis=1).astype(BF16)

    wq = w_q_up[0].reshape(Q_LORA, MLA_HEADS, MLA_QK)
    wq_rope = wq[:, :, MLA_NOPE:]
    w_qup = jnp.concatenate([wq[:, :, :MLA_NOPE].reshape(Q_LORA, -1),
                             pad_lanes(wq_rope).reshape(Q_LORA, -1),
                             pad_lanes(swap(wq_rope)).reshape(Q_LORA, -1)], axis=1).astype(BF16)
    wkv = w_kv_up[0].reshape(KV_LORA, MLA_HEADS, MLA_NOPE + MLA_V)
    w_kvup = jnp.concatenate([wkv[:, :, :MLA_NOPE].reshape(KV_LORA, -1),
                              wkv[:, :, MLA_NOPE:].reshape(KV_LORA, -1)], axis=1).astype(BF16)
    rope_gain = lambda t: jnp.stack([pad_lanes(t), pad_lanes(swap(t))]).astype(F32)
    g_qn, g_qr = row(g_q_norm[0, :MLA_NOPE]), rope_gain(g_q_norm[0, MLA_NOPE:])
    g_kn, g_kr = row(g_k_norm[0, :MLA_NOPE]), rope_gain(g_k_norm[0, MLA_NOPE:])
    invf = (ROPE_THETA ** (-jnp.arange(0, MLA_ROPE, 2, dtype=F32) / MLA_ROPE)).reshape(-1, 1)

    x2 = x.reshape(t_tok, d_model)
    pos = positions.reshape(1, t_tok).astype(F32)
    tm = min(512, s_len)
    qig, fgate, q_all, k_all, v = _input_stage(
        x2, pos, invf, row(g_mix_norm[0]), w_qig, w_f, w_mla, row(g_cq[0]), w_qup, row(g_ckv[0]),
        w_kvup, g_qn, g_qr, g_kn, g_kr, min(INPUT_TILE, s_len))

    a = _hgrn_stage(qig, fgate, lb_param.astype(F32), g_hgrn_out[0].reshape(1, HG_W).astype(F32),
                    batch, s_len)
    ob = _attn_stage(q_all, k_all, v, batch, s_len)

    wo = w_out[0].astype(BF16)
    out = _output_stage(x2, a, ob, row(g_mla_out[0]), wo[:HG_W], wo[HG_W:], row(g_ffn_norm[0]),
                        w_up[0].astype(BF16), w_down[0].astype(BF16), tm)
    return out.reshape(batch, s_len, d_model)
```

```python
import functools

import jax
import jax.numpy as jnp
from jax import lax
from jax.experimental import pallas as pl
from jax.experimental.pallas import tpu as pltpu

NORM_EPS = 1e-6
HG_HEADS = 4
HG_D = 128
HG_W = HG_HEADS * HG_D
MLA_HEADS = 4
MLA_NOPE = 128
MLA_ROPE = 64
MLA_V = 128
MLA_QK = MLA_NOPE + MLA_ROPE
Q_LORA = 384
KV_LORA = 256
ROPE_THETA = 10000.0
LANES = 128
QK_PAD = 2 * LANES

HG_CHUNK = 128
HG_LEAF = 8
HG_BLOCK = 64
HG_MAX_BLOCK_DECAY = 100.0
HG_UNROLL = 8
ATTN_SUB = 256
ATTN_HEADS = 2
ATTN_DEPTH = 1
INPUT_TILE = 512
INPUT_SUB = 256
LOG2_E = 1.4426950408889634
VMEM_LIMIT = 56 * 1024 * 1024

F32 = jnp.float32
BF16 = jnp.bfloat16


def _dot(a, b):
    return jnp.dot(a, b, preferred_element_type=F32)


def _dot_nt(a, b):
    return lax.dot_general(a, b, (((1,), (1,)), ((), ())), preferred_element_type=F32)


def _rms(t, gain):
    return t * lax.rsqrt(jnp.mean(t * t, axis=-1, keepdims=True) + NORM_EPS) * gain


def _sigmoid(t):
    return 1.0 / (1.0 + jnp.exp2(t * -LOG2_E))


def _store_slabs(ref, rows, t):
    for j in range(ref.shape[0]):
        ref[j, rows, :] = t[:, j * LANES:(j + 1) * LANES].astype(ref.dtype)


def _load_slabs(ref):
    return jnp.concatenate([ref[j] for j in range(ref.shape[0])], axis=1)


def _input_kernel(x_ref, pos_ref, invf_ref, g_mix_ref, w_qig_ref, w_f_ref, w_mla_ref, g_cq_ref,
                  w_qup_ref, g_ckv_ref, w_kvup_ref, g_qn_ref, g_qr_ref, g_kn_ref, g_kr_ref,
                  qig_ref, f_ref, qa_ref, ka_ref, v_ref):
    tm = x_ref.shape[0]
    sub = min(INPUT_SUB, tm)
    tiles = [dict(rows=slice(i * sub, (i + 1) * sub)) for i in range(tm // sub)]
    scale = MLA_QK ** -0.5 * LOG2_E

    def project(t):
        h = _rms(x_ref[t["rows"], :], g_mix_ref[...]).astype(BF16)
        _store_slabs(qig_ref, t["rows"], _dot(h, w_qig_ref[...]))
        _store_slabs(f_ref, t["rows"], _dot(h, w_f_ref[...]))
        t["m"] = _dot(h, w_mla_ref[...])

    def up_project(t):
        c_q = t["m"][:, :Q_LORA]
        c_kv = t["m"][:, Q_LORA:Q_LORA + KV_LORA]
        t["q"] = _dot(_rms(c_q, g_cq_ref[...]).astype(BF16), w_qup_ref[...])
        t["kv"] = _dot(_rms(c_kv, g_ckv_ref[...]).astype(BF16), w_kvup_ref[...])

    def rope_tables():
        ang = invf_ref[...] * pos_ref[...]
        c, s = jnp.cos(ang), jnp.sin(ang)
        z = jnp.zeros_like(c)
        return jnp.concatenate([c, c, z, z], axis=0).T, jnp.concatenate([-s, s, z, z], axis=0).T

    def heads(t, cos_t, sin_t):
        rows, q, kv = t["rows"], t["q"], t["kv"]
        _store_slabs(v_ref, rows, kv[:, MLA_HEADS * MLA_NOPE:])
        cos_r, sin_r = cos_t[rows, :], sin_t[rows, :]
        q_cos, q_sin = g_qr_ref[0:1] * cos_r, g_qr_ref[1:2] * sin_r
        rope_cols = Q_LORA + KV_LORA
        k_r = t["m"][:, rope_cols:rope_cols + LANES]
        k_pe = (k_r * (g_kr_ref[0:1] * cos_r)
                + t["m"][:, rope_cols + LANES:] * (g_kr_ref[1:2] * sin_r))
        sq_kr = k_r * k_r
        for hd in range(MLA_HEADS):
            lo, hi = hd * LANES, (hd + 1) * LANES
            q_n = q[:, lo:hi]
            q_r = q[:, MLA_HEADS * MLA_NOPE + lo:MLA_HEADS * MLA_NOPE + hi]
            q_r_swapped = q[:, 2 * MLA_HEADS * MLA_NOPE + lo:2 * MLA_HEADS * MLA_NOPE + hi]
            ssq = jnp.sum(q_n * q_n + q_r * q_r, axis=-1, keepdims=True)
            rstd = lax.rsqrt(ssq * (1.0 / MLA_QK) + NORM_EPS) * scale
            q_pe = q_r * q_cos + q_r_swapped * q_sin
            qa_ref[hd, rows, :] = jnp.concatenate([q_n * g_qn_ref[...] * rstd, q_pe * rstd],
                                                  axis=1).astype(BF16)
            k_n = kv[:, lo:hi]
            ssq = jnp.sum(k_n * k_n + sq_kr, axis=-1, keepdims=True)
            rstd = lax.rsqrt(ssq * (1.0 / MLA_QK) + NORM_EPS)
            ka_ref[hd, rows, :] = jnp.concatenate([k_n * g_kn_ref[...] * rstd, k_pe * rstd],
                                                  axis=1).astype(BF16)

    tables = None
    for i, t in enumerate(tiles):
        project(t)
        if i == 0:
            tables = rope_tables()
        else:
            heads(tiles[i - 1], *tables)
        up_project(t)
    heads(tiles[-1], *tables)


def _const_spec(shape):
    nd = len(shape)
    return pl.BlockSpec(shape, lambda *_: (0,) * nd, pipeline_mode=pl.Buffered(1))


def _input_stage(x2, pos, invf, g_mix, w_qig, w_f, w_mla, g_cq, w_qup, g_ckv, w_kvup,
                 g_qn, g_qr, g_kn, g_kr, tm):
    t_tok, d_model = x2.shape
    consts = (invf, g_mix, w_qig, w_f, w_mla, g_cq, w_qup, g_ckv, w_kvup, g_qn, g_qr, g_kn, g_kr)
    return pl.pallas_call(
        _input_kernel,
        grid=(t_tok // tm,),
        in_specs=[pl.BlockSpec((tm, d_model), lambda i: (i, 0)),
                  pl.BlockSpec((1, tm), lambda i: (0, i))] + [_const_spec(c.shape) for c in consts],
        out_specs=[pl.BlockSpec((3 * HG_HEADS, tm, HG_D), lambda i: (0, i, 0)),
                   pl.BlockSpec((2 * HG_HEADS, tm, HG_D), lambda i: (0, i, 0)),
                   pl.BlockSpec((MLA_HEADS, tm, QK_PAD), lambda i: (0, i, 0)),
                   pl.BlockSpec((MLA_HEADS, tm, QK_PAD), lambda i: (0, i, 0)),
                   pl.BlockSpec((MLA_HEADS, tm, MLA_V), lambda i: (0, i, 0))],
        out_shape=[jax.ShapeDtypeStruct((3 * HG_HEADS, t_tok, HG_D), BF16),
                   jax.ShapeDtypeStruct((2 * HG_HEADS, t_tok, HG_D), F32),
                   jax.ShapeDtypeStruct((MLA_HEADS, t_tok, QK_PAD), BF16),
                   jax.ShapeDtypeStruct((MLA_HEADS, t_tok, QK_PAD), BF16),
                   jax.ShapeDtypeStruct((MLA_HEADS, t_tok, MLA_V), BF16)],
        compiler_params=pltpu.CompilerParams(dimension_semantics=("arbitrary",),
                                             vmem_limit_bytes=VMEM_LIMIT),
        name="input_stage",
    )(x2, pos, *consts)


def _hg_levels(leaf):
    sizes = []
    half = HG_CHUNK // 2
    while half >= leaf:
        sizes.append(half)
        half //= 2
    return tuple(sizes)


def _hg_gate_terms(logit, lb):
    f = lb + (1.0 - lb) * _sigmoid(logit)
    g = jnp.log2(f)
    g_hi = g.astype(BF16)
    r1 = g - g_hi.astype(F32)
    g_mid = r1.astype(BF16)
    g_lo = (r1 - g_mid.astype(F32)).astype(BF16)
    return 1.0 - f, jnp.concatenate([g_hi, g_mid, g_lo], axis=0)


def _hg_block_decay(b, rev):
    n_blk = HG_CHUNK // HG_BLOCK
    edge = [b[i * HG_BLOCK:i * HG_BLOCK + 1] if rev else b[(i + 1) * HG_BLOCK - 1:(i + 1) * HG_BLOCK]
            for i in range(n_blk)]
    worst = edge[-1] if rev else edge[0]
    for i in range(1, n_blk):
        worst = jnp.minimum(worst, edge[i - 1] - edge[i] if rev else edge[i] - edge[i - 1])
    return worst


def _hg_pairs_blocked(jobs):
    c, n_blk = HG_CHUNK, HG_CHUNK // HG_BLOCK
    shape3 = (n_blk, HG_BLOCK, HG_D)
    zero_row = jnp.zeros((1, 1, HG_D), F32)
    zero_blk = jnp.zeros((HG_BLOCK, HG_D), BF16)
    rows = lambda pieces: jnp.concatenate([pieces.get(i, zero_blk) for i in range(n_blk)], axis=0)
    halves = [h // HG_BLOCK for h in _hg_levels(HG_BLOCK)]
    for j in jobs:
        rev = j["rev"]
        b3 = j["b"].reshape(shape3)
        if rev:
            end = b3[:, 0:1]
            ref = jnp.concatenate([end[1:], zero_row], axis=0)
        else:
            end = b3[:, HG_BLOCK - 1:HG_BLOCK]
            ref = jnp.concatenate([zero_row, end[:-1]], axis=0)
        order = [n_blk - 1 - i for i in range(n_blk)] if rev else list(range(n_blk))
        block_at = {o: i for i, o in enumerate(order)}
        b_tot = end[block_at[n_blk - 1]]
        q_b = j["q"].reshape(shape3) * jnp.exp2(b3 - ref)
        k_f = j["k"].reshape(shape3) * jnp.exp2(end - b3)
        q16 = [q_b[i].astype(BF16) for i in range(n_blk)]
        k16 = [k_f[i].astype(BF16) for i in range(n_blk)]
        j["dec"] = jnp.exp2(b_tot)
        j["q_in"] = rows({i: (q_b[i] * jnp.exp2(ref[i])).astype(BF16) for i in range(n_blk)})
        j["k_out"] = rows({i: (k_f[i] * jnp.exp2(b_tot - end[i])).astype(BF16) for i in range(n_blk)})
        q_x = [rows({i: q16[i]}) for i in range(n_blk)]
        k_x = [rows({i: (k_f[i] * jnp.exp2(ref[i] - end[i])).astype(BF16)}) for i in range(n_blk)]
        for half in halves:
            for first in range(0, n_blk, 2 * half):
                mid = end[block_at[first + half - 1]]
                q_l, k_l = {}, {}
                for i, o in enumerate(order):
                    if first + half <= o < first + 2 * half:
                        q_l[i] = (q16[i] if o == first + half
                                  else (q_b[i] * jnp.exp2(ref[i] - mid)).astype(BF16))
                    elif first <= o < first + half:
                        k_l[i] = (k16[i] if o == first + half - 1
                                  else (k_f[i] * jnp.exp2(mid - end[i])).astype(BF16))
                q_x.append(rows(q_l))
                k_x.append(rows(k_l))
        j["q_x"], j["k_x"] = jnp.concatenate(q_x, axis=1), jnp.concatenate(k_x, axis=1)

    pairs = list(zip(jobs[0::2], jobs[1::2]))
    for fwd, bwd in pairs:
        p = _dot_nt(jnp.concatenate([fwd["q_x"], bwd["q_x"]], axis=0),
                    jnp.concatenate([fwd["k_x"], bwd["k_x"]], axis=0))
        fwd["p"] = (p[:c, :c] * fwd["causal_mask"]).astype(BF16)
        bwd["p"] = (p[c:, c:] * bwd["causal_mask"]).astype(BF16)
    for fwd, bwd in pairs:
        st_add = _dot(fwd["v"].T.astype(BF16), jnp.concatenate([fwd["k_out"], bwd["k_out"]], axis=1))
        fwd["st_add"], bwd["st_add"] = st_add[:, :HG_D], st_add[:, HG_D:]
    for fwd, bwd in pairs:
        o = _dot(jnp.concatenate([fwd["p"], bwd["p"]], axis=0), fwd["v"].astype(BF16))
        fwd["o"], bwd["o"] = o[:c], o[c:]


def _hg_chunks(jobs, lvl_masks, blocked):
    c = HG_CHUNK
    if blocked:
        _hg_pairs_blocked(jobs)
        return [(j["o"], j["q_in"], j["st_add"], j["dec"]) for j in jobs]

    for j in jobs:
        q, k, b = j["q"], j["k"], j["b"]
        b_tot = b[0:1] if j["rev"] else b[c - 1:c]
        j["dec"] = jnp.exp2(b_tot)
        j["q_in"] = (q * jnp.exp2(b)).astype(BF16)
        k_out = (k * jnp.exp2(b_tot - b)).astype(BF16)
        j["st_add"] = _dot(j["v"].T.astype(BF16), k_out)
        j["p"] = None

    for lvl, half in enumerate(_hg_levels(HG_LEAF)):
        n_grp = c // (2 * half)
        shape3 = (n_grp, 2 * half, HG_D)
        zero = jnp.zeros((n_grp, half, HG_D), F32)
        for j in jobs:
            b3, q3, k3 = j["b"].reshape(shape3), j["q"].reshape(shape3), j["k"].reshape(shape3)
            if j["rev"]:
                mid = b3[:, half:half + 1]
                q_l = jnp.concatenate([q3[:, :half] * jnp.exp2(b3[:, :half] - mid), zero], axis=1)
                k_l = jnp.concatenate([zero, k3[:, half:] * jnp.exp2(mid - b3[:, half:])], axis=1)
            else:
                mid = b3[:, half - 1:half]
                q_l = jnp.concatenate([zero, q3[:, half:] * jnp.exp2(b3[:, half:] - mid)], axis=1)
                k_l = jnp.concatenate([k3[:, :half] * jnp.exp2(mid - b3[:, :half]), zero], axis=1)
            p_l = _dot_nt(q_l.reshape(c, HG_D).astype(BF16), k_l.reshape(c, HG_D).astype(BF16))
            p_l = p_l * lvl_masks[lvl] if n_grp > 1 else p_l
            j["p"] = p_l if j["p"] is None else j["p"] + p_l

    for j in jobs:
        j["kpad"][HG_LEAF:HG_LEAF + c, :] = j["k"]
        j["bpad"][HG_LEAF:HG_LEAF + c, :] = j["b"]
    for j in jobs:
        w = [(j["q"] * j["k"]).astype(BF16)]
        for d in range(1, HG_LEAF):
            lo = HG_LEAF + d if j["rev"] else HG_LEAF - d
            k_s = j["kpad"][lo:lo + c, :]
            b_s = j["bpad"][lo:lo + c, :]
            w.append((j["q"] * k_s * jnp.exp2(j["b"] - b_s)).astype(BF16))
        j["row_sums"] = _dot(jnp.concatenate(w, axis=1), j["spread"])
    for j in jobs:
        j["p_leaf"] = (pltpu.roll(j["row_sums"], 0, 1, stride=1, stride_axis=0)
                       * j["leaf_mask"])

    outs = []
    for j in jobs:
        p = (j["p"] + j["p_leaf"]).astype(BF16)
        outs.append((_dot(p, j["v"].astype(BF16)), j["q_in"], j["st_add"], j["dec"]))
    return outs


def _hgrn_kernel(q_ref, i_ref, g_ref, ff_ref, fb_ref, lbp_ref, gain_ref, tri_ref, lvl_ref, leaf_ref,
                 causal_ref, spread_ref, out_ref, k_ref, b_ref, o_ref, qin_ref, add_ref, dec_ref,
                 st_ref, kpad_ref, bpad_ref):
    s_len = q_ref.shape[0]
    n_chunk = s_len // HG_CHUNK
    logit_refs = (ff_ref, fb_ref)
    chunk_rows = lambda ci: pl.ds(pl.multiple_of(ci * HG_CHUNK, HG_CHUNK), HG_CHUNK)

    def lower_bound(d):
        p = lbp_ref[d]
        e = jnp.exp(p - jnp.max(p, axis=0, keepdims=True))
        return e[0:1] / jnp.sum(e, axis=0, keepdims=True)

    lbs = (lower_bound(0), lower_bound(1))

    def gates(n, worst):
        rows = [chunk_rows(n * HG_UNROLL + u) for u in range(HG_UNROLL)]
        for d in range(2):
            terms = []
            for r in rows:
                k, g3 = _hg_gate_terms(logit_refs[d][r, :], lbs[d])
                k_ref[d, r, :] = k
                terms.append(g3)
            b_all = _dot(tri_ref[d], jnp.concatenate(terms, axis=1))
            for u, r in enumerate(rows):
                b = b_all[:, u * HG_D:(u + 1) * HG_D]
                b_ref[d, r, :] = b
                worst = jnp.minimum(worst, _hg_block_decay(b, bool(d)))
        return worst

    worst = lax.fori_loop(0, n_chunk // HG_UNROLL, gates, jnp.zeros((1, HG_D), F32))
    blocks_representable = jnp.min(worst) >= -HG_MAX_BLOCK_DECAY

    def within_chunks(blocked):
        def body(n, carry):
            lvl_masks = [lvl_ref[l] for l in range(lvl_ref.shape[0])]
            jobs = []
            for u in range(HG_UNROLL):
                ci = n * HG_UNROLL + u
                rows = chunk_rows(ci)
                for d, rev in enumerate((False, True)):
                    slot = d * HG_UNROLL + u
                    jobs.append(dict(
                        ci=ci, rows=rows, d=d, rev=rev, q=q_ref[rows, :].astype(F32),
                        k=k_ref[d, rows, :], b=b_ref[d, rows, :], v=i_ref[rows, :].astype(F32),
                        kpad=kpad_ref.at[slot], bpad=bpad_ref.at[slot], leaf_mask=leaf_ref[d],
                        causal_mask=causal_ref[d], spread=spread_ref[d]))
            for j, (o, q_in, st_add, dec) in zip(jobs, _hg_chunks(jobs, lvl_masks, blocked)):
                o_ref[j["d"], j["rows"], :] = o
                qin_ref[j["rows"], j["d"] * HG_D:(j["d"] + 1) * HG_D] = q_in
                add_ref[j["d"], j["ci"]] = st_add
                dec_ref[j["d"], j["ci"]] = dec
            return carry
        return body

    @pl.when(blocks_representable)
    def _():
        lax.fori_loop(0, n_chunk // HG_UNROLL, within_chunks(True), 0)

    @pl.when(jnp.logical_not(blocks_representable))
    def _():
        kpad_ref[...] = jnp.zeros_like(kpad_ref)
        bpad_ref[...] = jnp.zeros_like(bpad_ref)
        lax.fori_loop(0, n_chunk // HG_UNROLL, within_chunks(False), 0)

    def across_chunks(n, states):
        new_states = []
        for d, st in enumerate(states):
            ci = (n_chunk - 1 - n) if d else n
            st_ref[ci, :, d * HG_D:(d + 1) * HG_D] = st.astype(BF16)
            new_states.append(dec_ref[d, ci] * st + add_ref[d, ci])
        return tuple(new_states)

    zero_state = jnp.zeros((HG_D, HG_D), F32)
    lax.fori_loop(0, n_chunk, across_chunks, (zero_state, zero_state), unroll=True)

    def finish(n, carry):
        rows = chunk_rows(n)
        o = o_ref[0, rows, :] + o_ref[1, rows, :] + _dot_nt(qin_ref[rows, :], st_ref[n])
        y = _rms(o, gain_ref[...])
        gate = g_ref[rows, :].astype(F32)
        out_ref[rows, :] = (y * gate * _sigmoid(gate)).astype(BF16)
        return carry

    lax.fori_loop(0, n_chunk, finish, 0, unroll=16)


def _hgrn_stage(qig, fgate, lb_param, gain, batch, s_len):
    assert HG_CHUNK == LANES, "the leaf path rotates each row of a (C, C) tile by its row index"
    t_tok = qig.shape[1]
    c = HG_CHUNK
    r = jnp.arange(c)
    t_i, s_i = r[:, None], r[None, :]
    lower = s_i <= t_i
    causal = jnp.stack([lower, lower.T])
    tri = jnp.tile(causal, (1, 1, 3)).astype(BF16)
    lvl_masks = jnp.stack([t_i // (2 * h) == s_i // (2 * h) for h in _hg_levels(HG_LEAF)]).astype(F32)
    leaf_mask = (causal & (t_i // HG_LEAF == s_i // HG_LEAF)).astype(F32)
    dist = jnp.repeat(jnp.arange(HG_LEAF), HG_D)[:, None]
    spread = jnp.stack([(-s_i) % HG_LEAF == dist, s_i % HG_LEAF == dist]).astype(BF16)
    blk = lambda col: pl.BlockSpec((None, s_len, HG_D),
                                   lambda b, h, col=col: (col * HG_HEADS + h, b, 0))
    consts = (tri, lvl_masks, leaf_mask, causal.astype(F32), spread)
    seq = lambda dtype: pltpu.VMEM((2, s_len, HG_D), dtype)
    return pl.pallas_call(
        _hgrn_kernel,
        grid=(batch, HG_HEADS),
        in_specs=[blk(0), blk(1), blk(2), blk(0), blk(1),
                  pl.BlockSpec((2, 2, HG_D), lambda b, h: (0, 0, h)),
                  pl.BlockSpec((1, HG_D), lambda b, h: (0, h))] + [_const_spec(t.shape) for t in consts],
        out_specs=pl.BlockSpec((None, s_len, HG_D), lambda b, h: (h, b, 0)),
        out_shape=jax.ShapeDtypeStruct((HG_HEADS, t_tok, HG_D), BF16),
        scratch_shapes=[seq(F32), seq(F32), seq(F32), pltpu.VMEM((s_len, 2 * HG_D), BF16),
                        pltpu.VMEM((2, s_len // c, HG_D, HG_D), F32),
                        pltpu.VMEM((2, s_len // c, 1, HG_D), F32),
                        pltpu.VMEM((s_len // c, HG_D, 2 * HG_D), BF16),
                        pltpu.VMEM((2 * HG_UNROLL, c + 2 * HG_LEAF, HG_D), F32),
                        pltpu.VMEM((2 * HG_UNROLL, c + 2 * HG_LEAF, HG_D), F32)],
        compiler_params=pltpu.CompilerParams(dimension_semantics=("arbitrary", "arbitrary"),
                                             vmem_limit_bytes=VMEM_LIMIT),
        name="hgrn_stage",
    )(qig, qig, qig, fgate, fgate, lb_param, gain, *consts)


def _attn_kernel(q_ref, k_ref, v_ref, o_ref):
    n_heads, s_len, _ = q_ref.shape
    sub = min(ATTN_SUB, s_len)
    items = [(h, slice(i * sub, (i + 1) * sub)) for h in range(n_heads) for i in range(s_len // sub)]
    scores = lambda h, rows: _dot_nt(q_ref[h, rows, :], k_ref[h])
    pending = [scores(*item) for item in items[:ATTN_DEPTH]]
    for n, (h, rows) in enumerate(items):
        s = pending.pop(0)
        if n + ATTN_DEPTH < len(items):
            pending.append(scores(*items[n + ATTN_DEPTH]))
        p = jnp.exp2(s - jnp.max(s, axis=-1, keepdims=True))
        o = _dot(p.astype(BF16), v_ref[h])
        o_ref[h, rows, :] = (o * (1.0 / jnp.sum(p, axis=-1, keepdims=True))).astype(BF16)


def _attn_stage(q_all, k_all, v, batch, s_len):
    t_tok = v.shape[1]
    n_grp = MLA_HEADS // ATTN_HEADS
    return pl.pallas_call(
        _attn_kernel,
        grid=(batch, n_grp),
        in_specs=[pl.BlockSpec((ATTN_HEADS, s_len, QK_PAD), lambda b, g: (g, b, 0)),
                  pl.BlockSpec((ATTN_HEADS, s_len, QK_PAD), lambda b, g: (g, b, 0)),
                  pl.BlockSpec((ATTN_HEADS, s_len, MLA_V), lambda b, g: (g, b, 0))],
        out_specs=pl.BlockSpec((ATTN_HEADS, s_len, MLA_V), lambda b, g: (g, b, 0)),
        out_shape=jax.ShapeDtypeStruct((MLA_HEADS, t_tok, MLA_V), BF16),
        compiler_params=pltpu.CompilerParams(dimension_semantics=("arbitrary",) * 2,
                                             vmem_limit_bytes=VMEM_LIMIT),
        name="attn_stage",
    )(q_all, k_all, v)


def _output_kernel(x_ref, a_ref, ob_ref, g_mla_ref, w_oa_ref, w_ob_ref, g_ffn_ref, w_up_ref,
                   w_dn_ref, out_ref):
    ob = _rms(_load_slabs(ob_ref).astype(F32), g_mla_ref[...]).astype(BF16)
    x1 = x_ref[...] + _dot(_load_slabs(a_ref), w_oa_ref[...]) + _dot(ob, w_ob_ref[...])
    h = _rms(x1, g_ffn_ref[...]).astype(BF16)
    up = jnp.maximum(_dot(h, w_up_ref[...]), 0.0)
    out_ref[...] = x1 + _dot((up * up).astype(BF16), w_dn_ref[...])


def _output_stage(x2, a, ob, g_mla, w_oa, w_ob, g_ffn, w_up, w_dn, tm):
    t_tok, d_model = x2.shape
    consts = (g_mla, w_oa, w_ob, g_ffn, w_up, w_dn)
    return pl.pallas_call(
        _output_kernel,
        grid=(t_tok // tm,),
        in_specs=[pl.BlockSpec((tm, d_model), lambda i: (i, 0)),
                  pl.BlockSpec((HG_HEADS, tm, HG_D), lambda i: (0, i, 0)),
                  pl.BlockSpec((MLA_HEADS, tm, MLA_V), lambda i: (0, i, 0))]
                 + [_const_spec(c.shape) for c in consts],
        out_specs=pl.BlockSpec((tm, d_model), lambda i: (i, 0)),
        out_shape=jax.ShapeDtypeStruct((t_tok, d_model), F32),
        compiler_params=pltpu.CompilerParams(dimension_semantics=("arbitrary",),
                                             vmem_limit_bytes=VMEM_LIMIT),
        name="output_stage",
    )(x2, a, ob, *consts)


def kernel(x, positions, g_mix_norm, w_in, lb_param, g_hgrn_out, g_cq, w_q_up, g_ckv, w_kv_up,
           g_q_norm, g_k_norm, g_mla_out, w_out, g_ffn_norm, w_up, w_down):
    batch, s_len, d_model = x.shape
    assert g_mix_norm.shape[0] == 1 and lb_param.shape[1] == 2, "one layer"
    assert s_len % (HG_CHUNK * HG_UNROLL) == 0
    t_tok = batch * s_len
    row = lambda t: t.reshape(1, -1).astype(F32)

    w = w_in[0]
    o_q, o_ff, o_fb, o_i, o_g, o_cq = (HG_W * n for n in range(6))
    o_ckv = o_cq + Q_LORA
    o_kr = o_ckv + KV_LORA
    w_qig = jnp.concatenate([w[:, o_q:o_ff], w[:, o_i:o_g], w[:, o_g:o_cq]], axis=1).astype(BF16)
    w_f = w[:, o_ff:o_i].astype(BF16)
    half = MLA_ROPE // 2
    swap = lambda t: jnp.concatenate([t[..., half:], t[..., :half]], axis=-1)
    pad_lanes = lambda t: jnp.pad(t, [(0, 0)] * (t.ndim - 1) + [(0, LANES - MLA_ROPE)])
    w_kr = w[:, o_kr:]
    w_mla = jnp.concatenate([w[:, o_cq:o_kr], pad_lanes(w_kr), pad_lanes(swap(w_kr))],
                            axis=1).astype(BF16)

    wq = w_q_up[0].reshape(Q_LORA, MLA_HEADS, MLA_QK)
    wq_rope = wq[:, :, MLA_NOPE:]
    w_qup = jnp.concatenate([wq[:, :, :MLA_NOPE].reshape(Q_LORA, -1),
                             pad_lanes(wq_rope).reshape(Q_LORA, -1),
                             pad_lanes(swap(wq_rope)).reshape(Q_LORA, -1)], axis=1).astype(BF16)
    wkv = w_kv_up[0].reshape(KV_LORA, MLA_HEADS, MLA_NOPE + MLA_V)
    w_kvup = jnp.concatenate([wkv[:, :, :MLA_NOPE].reshape(KV_LORA, -1),
                              wkv[:, :, MLA_NOPE:].reshape(KV_LORA, -1)], axis=1).astype(BF16)
    rope_gain = lambda t: jnp.stack([pad_lanes(t), pad_lanes(swap(t))]).astype(F32)
    g_qn, g_qr = row(g_q_norm[0, :MLA_NOPE]), rope_gain(g_q_norm[0, MLA_NOPE:])
    g_kn, g_kr = row(g_k_norm[0, :MLA_NOPE]), rope_gain(g_k_norm[0, MLA_NOPE:])
    invf = (ROPE_THETA ** (-jnp.arange(0, MLA_ROPE, 2, dtype=F32) / MLA_ROPE)).reshape(-1, 1)

    x2 = x.reshape(t_tok, d_model)
    pos = positions.reshape(1, t_tok).astype(F32)
    tm = min(512, s_len)
    qig, fgate, q_all, k_all, v = _input_stage(
        x2, pos, invf, row(g_mix_norm[0]), w_qig, w_f, w_mla, row(g_cq[0]), w_qup, row(g_ckv[0]),
        w_kvup, g_qn, g_qr, g_kn, g_kr, min(INPUT_TILE, s_len))

    a = _hgrn_stage(qig, fgate, lb_param.astype(F32), g_hgrn_out[0].reshape(1, HG_W).astype(F32),
                    batch, s_len)
    ob = _attn_stage(q_all, k_all, v, batch, s_len)

    wo = w_out[0].astype(BF16)
    out = _output_stage(x2, a, ob, row(g_mla_out[0]), wo[:HG_W], wo[HG_W:], row(g_ffn_norm[0]),
                        w_up[0].astype(BF16), w_down[0].astype(BF16), tm)
    return out.reshape(batch, s_len, d_model)
```
